```python
import jax, jax.numpy as jnp
from jax import lax
import numpy as np

D_MODEL = 1024
BATCH = 4
SEQ = 4096
DEPTH = 4

CTX_LEN = 256
GRID_W = 64
ROPE_THETA = 10000.0
EPS = 1e-6
BLOCK = 128
HEAD_DIM = 64
D_MIX = D_MODEL
MLA_HEADS = 4
MLA_NOPE = 64
MLA_ROPE = 32
MLA_V = 64
MLA_Q_RANK = 256
MLA_KV_RANK = 128
GQA_HEADS = 4
GQA_KV_HEADS = 2
SWA_HEADS = 4
SWA_KV_HEADS = 2
WINDOW = 128
SSD_HEADS = 4
SSD_HEAD_DIM = 64
SSD_GROUPS = 2
SSD_STATE = 128
SSD_CONV = 5
SSD_CHUNK = 128
SSD_INNER = SSD_HEADS * SSD_HEAD_DIM
SSD_CONV_DIM = SSD_INNER + 2 * SSD_GROUPS * SSD_STATE
A_WIDTHS = (MLA_Q_RANK, MLA_KV_RANK, MLA_ROPE, MLA_HEADS * MLA_V)
B_WIDTHS = (GQA_HEADS * HEAD_DIM, GQA_KV_HEADS * HEAD_DIM, GQA_KV_HEADS * HEAD_DIM, GQA_HEADS * HEAD_DIM)
C_WIDTHS = (SWA_HEADS * HEAD_DIM, SWA_KV_HEADS * HEAD_DIM, SWA_KV_HEADS * HEAD_DIM, SWA_HEADS * HEAD_DIM)
D_WIDTHS = (SSD_INNER, SSD_CONV_DIM, 2 * SSD_HEADS)
IN_WIDTHS = A_WIDTHS + B_WIDTHS + C_WIDTHS + D_WIDTHS
IN_COLS = sum(IN_WIDTHS)

kernel_name = 'hybrid_head_group_flow_block'


def rmsnorm(x, w):
    xf = x.astype(jnp.float32)
    y = xf * lax.rsqrt(jnp.mean(xf * xf, axis=-1, keepdims=True) + EPS)
    return (y * w.astype(jnp.float32)).astype(x.dtype)


def split_cols(p, widths):
    return jnp.split(p, [int(i) for i in np.cumsum(widths)[:-1]], axis=-1)


def axial_rope(n_tokens, rot_dim):
    rows = n_tokens // GRID_W
    row = jnp.repeat(jnp.arange(rows, dtype=jnp.float32), GRID_W)
    col = jnp.tile(jnp.arange(GRID_W, dtype=jnp.float32), rows)
    n_freq = rot_dim // 4
    inv_freq = ROPE_THETA ** (-jnp.arange(n_freq, dtype=jnp.float32) / n_freq)
    ang = jnp.concatenate([row[:, None] * inv_freq, col[:, None] * inv_freq], axis=-1)
    return jnp.cos(ang), jnp.sin(ang)


def apply_rope(x, cos, sin):
    half = x.shape[-1] // 2
    x1, x2 = x[..., :half], x[..., half:]
    c, s = cos[:, None, :], sin[:, None, :]
    return jnp.concatenate([x1 * c - x2 * s, x1 * s + x2 * c], axis=-1).astype(x.dtype)


def blocked_attention(q, k, v, scale):
    bsz, n = q.shape[:2]
    nblk = n // BLOCK
    qb = q.reshape(bsz, nblk, BLOCK, *q.shape[2:]).swapaxes(0, 1)

    def one_block(q_blk):
        s = jnp.einsum('bqhgd,bkhd->bhgqk', q_blk, k).astype(jnp.float32) * scale
        p = jax.nn.softmax(s, axis=-1).astype(v.dtype)
        return jnp.einsum('bhgqk,bkhd->bqhgd', p, v)

    out = lax.map(one_block, qb)
    return out.swapaxes(0, 1).reshape(bsz, n, -1)


def dense_sink_attention(q, k, v, sink, scale):
    bsz, n, kvh, g, _ = q.shape
    s = jnp.einsum('bqhgd,bkhd->bhgqk', q, k).astype(jnp.float32) * scale
    sink_col = jnp.broadcast_to(sink.astype(jnp.float32).reshape(kvh, g)[None, :, :, None, None], s.shape[:-1] + (1,))
    p = jax.nn.softmax(jnp.concatenate([s, sink_col], axis=-1), axis=-1)[..., :-1].astype(v.dtype)
    return jnp.einsum('bhgqk,bkhd->bqhgd', p, v).reshape(bsz, n, -1)


def window_sink_attention(q, k, v, k_ctx, v_ctx, sink, scale):
    bsz, n, kvh, g, d = q.shape
    nblk = n // BLOCK
    pad = ((0, 0), (BLOCK, BLOCK), (0, 0), (0, 0))
    kp = jnp.pad(k, pad).reshape(bsz, nblk + 2, BLOCK, kvh, d)
    vp = jnp.pad(v, pad).reshape(bsz, nblk + 2, BLOCK, kvh, d)
    k_band = jnp.concatenate([kp[:, :-2], kp[:, 1:-1], kp[:, 2:]], axis=2)
    v_band = jnp.concatenate([vp[:, :-2], vp[:, 1:-1], vp[:, 2:]], axis=2)
    qb = q.reshape(bsz, nblk, BLOCK, kvh, g, d)
    s_loc = jnp.einsum('bnqhgd,bnkhd->bnhgqk', qb, k_band).astype(jnp.float32) * scale
    s_ctx = jnp.einsum('bnqhgd,bkhd->bnhgqk', qb, k_ctx).astype(jnp.float32) * scale
    blk = jnp.arange(nblk)[:, None, None] * BLOCK
    q_pos = blk + jnp.arange(BLOCK)[None, :, None]
    k_pos = blk - BLOCK + jnp.arange(3 * BLOCK)[None, None, :]
    valid = (jnp.abs(k_pos - q_pos) <= WINDOW) & (k_pos >= 0) & (k_pos < n)
    s_loc = jnp.where(valid[None, :, None, None], s_loc, -jnp.inf)
    sink_col = jnp.broadcast_to(sink.astype(jnp.float32).reshape(kvh, g)[None, None, :, :, None, None], s_ctx.shape[:-1] + (1,))
    p = jax.nn.softmax(jnp.concatenate([s_ctx, s_loc, sink_col], axis=-1), axis=-1).astype(v.dtype)
    m = k_ctx.shape[1]
    out = (jnp.einsum('bnhgqk,bkhd->bnqhgd', p[..., :m], v_ctx)
           + jnp.einsum('bnhgqk,bnkhd->bnqhgd', p[..., m:m + 3 * BLOCK], v_band))
    return out.reshape(bsz, n, kvh * g * d)


def centred_dwconv(u, w, b):
    k = w.shape[0]
    out = lax.conv_general_dilated(u, w[:, None, :].astype(u.dtype), window_strides=(1,),
                                   padding=[(k // 2, k // 2)], dimension_numbers=('NWC', 'WIO', 'NWC'),
                                   feature_group_count=u.shape[-1])
    return out + b


def ssd_scan(x, dt, bm, cm, a, init, with_output):
    bsz, n, nh, hp = x.shape
    nc = n // SSD_CHUNK
    rep = nh // bm.shape[2]
    bh = jnp.repeat(bm, rep, axis=2).reshape(bsz, nc, SSD_CHUNK, nh, -1)
    xdt = (x * dt[..., None]).reshape(bsz, nc, SSD_CHUNK, nh, hp)
    a_cum = jnp.cumsum((dt * a).reshape(bsz, nc, SSD_CHUNK, nh), axis=2)
    a_tot = a_cum[:, :, -1]
    states = jnp.einsum('bckhn,bckhp->bchpn', bh * jnp.exp(a_tot[:, :, None] - a_cum)[..., None], xdt)

    def step(s, inp):
        st, at = inp
        return s * jnp.exp(at)[:, :, None, None] + st, s

    final, prev = lax.scan(step, init, (jnp.moveaxis(states, 1, 0), jnp.moveaxis(a_tot, 1, 0)))
    if not with_output:
        return None, final
    prev = jnp.moveaxis(prev, 0, 1)
    ch = jnp.repeat(cm, rep, axis=2).reshape(bsz, nc, SSD_CHUNK, nh, -1)
    seg = a_cum[:, :, :, None, :] - a_cum[:, :, None, :, :]
    lower = jnp.tril(jnp.ones((SSD_CHUNK, SSD_CHUNK), dtype=bool))
    decay = jnp.exp(jnp.where(lower[None, None, :, :, None], seg, -jnp.inf))
    scores = jnp.einsum('bcqhn,bckhn->bcqkh', ch, bh) * decay
    y = (jnp.einsum('bcqkh,bckhp->bcqhp', scores, xdt)
         + jnp.einsum('bcqhn,bchpn->bcqhp', ch, prev) * jnp.exp(a_cum)[..., None])
    return y.reshape(bsz, n, nh, hp), final


def mla_keys(ckv, k_rope, kv_norm_w, w_kv_up, rope):
    bsz, n = ckv.shape[:2]
    kv = (rmsnorm(ckv, kv_norm_w) @ w_kv_up).reshape(bsz, n, MLA_HEADS, MLA_NOPE + MLA_V)
    kr = k_rope[:, :, None, :]
    if rope is not None:
        kr = apply_rope(kr, *rope)
    k = jnp.concatenate([kv[..., :MLA_NOPE], jnp.broadcast_to(kr, (bsz, n, MLA_HEADS, MLA_ROPE))], axis=-1)
    return k, kv[..., MLA_NOPE:]


def mla_queries(cq, q_norm_w, w_q_up, rope):
    bsz, n = cq.shape[:2]
    q = (rmsnorm(cq, q_norm_w) @ w_q_up).reshape(bsz, n, MLA_HEADS, MLA_NOPE + MLA_ROPE)
    if rope is not None:
        q = jnp.concatenate([q[..., :MLA_NOPE], apply_rope(q[..., MLA_NOPE:], *rope)], axis=-1)
    return q[:, :, :, None, :]


def mla_branch(lat, ctx, q_norm_w, kv_norm_w, w_q_up, w_kv_up, rope, need_ctx_out):
    cq, ckv, kr, gate = lat
    cq_c, ckv_c, kr_c, gate_c = ctx
    scale = (MLA_NOPE + MLA_ROPE) ** -0.5
    k_l, v_l = mla_keys(ckv, kr, kv_norm_w, w_kv_up, rope)
    k_c, v_c = mla_keys(ckv_c, kr_c, kv_norm_w, w_kv_up, None)
    q_l = mla_queries(cq, q_norm_w, w_q_up, rope)
    out_l = blocked_attention(q_l, jnp.concatenate([k_c, k_l], 1), jnp.concatenate([v_c, v_l], 1), scale) * jax.nn.silu(gate)
    out_c = None
    if need_ctx_out:
        q_c = mla_queries(cq_c, q_norm_w, w_q_up, None)
        out_c = blocked_attention(q_c, k_c, v_c, scale) * jax.nn.silu(gate_c)
    return out_l, out_c


def global_gqa_branch(lat, ctx, q_norm_w, k_norm_w, rope, need_ctx_out):
    q, k, v, gate = lat
    q_c, k_c, v_c, gate_c = ctx
    bsz, n = q.shape[:2]
    m = k_c.shape[1]
    g = GQA_HEADS // GQA_KV_HEADS
    scale = HEAD_DIM ** -0.5
    q = apply_rope(rmsnorm(q.reshape(bsz, n, GQA_HEADS, HEAD_DIM), q_norm_w), *rope).reshape(bsz, n, GQA_KV_HEADS, g, HEAD_DIM)
    k = apply_rope(rmsnorm(k.reshape(bsz, n, GQA_KV_HEADS, HEAD_DIM), k_norm_w), *rope)
    v = v.reshape(bsz, n, GQA_KV_HEADS, HEAD_DIM)
    k_c = rmsnorm(k_c.reshape(bsz, m, GQA_KV_HEADS, HEAD_DIM), k_norm_w)
    v_c = v_c.reshape(bsz, m, GQA_KV_HEADS, HEAD_DIM)
    out_l = blocked_attention(q, jnp.concatenate([k_c, k], 1), jnp.concatenate([v_c, v], 1), scale) * jax.nn.silu(gate)
    out_c = None
    if need_ctx_out:
        q_c = rmsnorm(q_c.reshape(bsz, m, GQA_HEADS, HEAD_DIM), q_norm_w).reshape(bsz, m, GQA_KV_HEADS, g, HEAD_DIM)
        out_c = blocked_attention(q_c, k_c, v_c, scale) * jax.nn.silu(gate_c)
    return out_l, out_c


def window_gqa_branch(lat, ctx, sink, rope, need_ctx_out):
    q, k, v, gate = lat
    q_c, k_c, v_c, gate_c = ctx
    bsz, n = q.shape[:2]
    m = k_c.shape[1]
    g = SWA_HEADS // SWA_KV_HEADS
    scale = HEAD_DIM ** -0.5
    q = apply_rope(q.reshape(bsz, n, SWA_HEADS, HEAD_DIM), *rope).reshape(bsz, n, SWA_KV_HEADS, g, HEAD_DIM)
    k = apply_rope(k.reshape(bsz, n, SWA_KV_HEADS, HEAD_DIM), *rope)
    v = v.reshape(bsz, n, SWA_KV_HEADS, HEAD_DIM)
    k_c = k_c.reshape(bsz, m, SWA_KV_HEADS, HEAD_DIM)
    v_c = v_c.reshape(bsz, m, SWA_KV_HEADS, HEAD_DIM)
    out_l = window_sink_attention(q, k, v, k_c, v_c, sink, scale) * jax.nn.silu(gate)
    out_c = None
    if need_ctx_out:
        q_c = q_c.reshape(bsz, m, SWA_KV_HEADS, g, HEAD_DIM)
        out_c = dense_sink_attention(q_c, k_c, v_c, sink, scale) * jax.nn.silu(gate_c)
    return out_l, out_c


def ssd_prep(xbc, conv_w, conv_b):
    u = jax.nn.silu(centred_dwconv(xbc, conv_w, conv_b))
    xs, bm, cm = jnp.split(u, [SSD_INNER, SSD_INNER + SSD_GROUPS * SSD_STATE], axis=-1)
    bsz, n = xs.shape[:2]
    return (xs.reshape(bsz, n, SSD_HEADS, SSD_HEAD_DIM), bm.reshape(bsz, n, SSD_GROUPS, SSD_STATE),
            cm.reshape(bsz, n, SSD_GROUPS, SSD_STATE))


def ssd_branch(lat, ctx, conv_w, conv_b, a_log, dt_bias, d_skip, norm_w, need_ctx_out):
    z_l, xbc_l, dt_l = lat
    z_c, xbc_c, dt_c = ctx
    x_l, b_l, c_l = ssd_prep(xbc_l, conv_w, conv_b)
    x_c, b_c, c_c = ssd_prep(xbc_c, conv_w, conv_b)
    bsz = x_l.shape[0]
    skip = d_skip.astype(jnp.float32)[None, None, :, None]
    y_l = skip * x_l
    y_c = skip * x_c if need_ctx_out else None
    for direction in range(2):
        a = -jnp.exp(a_log[direction].astype(jnp.float32))
        sl = slice(direction * SSD_HEADS, (direction + 1) * SSD_HEADS)
        bias = dt_bias[direction].astype(jnp.float32)
        seq_l = [x_l, jax.nn.softplus(dt_l[..., sl].astype(jnp.float32) + bias), b_l, c_l]
        seq_c = [x_c, jax.nn.softplus(dt_c[..., sl].astype(jnp.float32) + bias), b_c, c_c]
        if direction == 1:
            seq_l = [jnp.flip(t, axis=1) for t in seq_l]
            seq_c = [jnp.flip(t, axis=1) for t in seq_c]
        init = jnp.zeros((bsz, SSD_HEADS, SSD_HEAD_DIM, SSD_STATE), jnp.float32)
        yc_dir, s_ctx = ssd_scan(*seq_c, a, init, need_ctx_out)
        yl_dir, _ = ssd_scan(*seq_l, a, s_ctx, True)
        if direction == 1:
            yl_dir = jnp.flip(yl_dir, axis=1)
            yc_dir = jnp.flip(yc_dir, axis=1) if need_ctx_out else None
        y_l = y_l + yl_dir
        if need_ctx_out:
            y_c = y_c + yc_dir
    n = x_l.shape[1]
    out_l = rmsnorm(y_l.reshape(bsz, n, SSD_INNER) * jax.nn.silu(z_l), norm_w).astype(z_l.dtype)
    out_c = None
    if need_ctx_out:
        m = x_c.shape[1]
        out_c = rmsnorm(y_c.reshape(bsz, m, SSD_INNER) * jax.nn.silu(z_c), norm_w).astype(z_c.dtype)
    return out_l, out_c


def hybrid_layer(x, xc, c, c_ctx, norm_w, w_mod, b_mod, w_in, mla_q_norm, mla_kv_norm, mla_w_q_up, mla_w_kv_up,
                 gqa_q_norm, gqa_k_norm, swa_sink, ssd_conv_w, ssd_conv_b, ssd_a_log, ssd_dt_bias, ssd_d,
                 ssd_norm_w, w_out, rope_mla, rope_head, need_ctx_out):
    shift, scale, gate = jnp.split(jax.nn.silu(c) @ w_mod + b_mod, 3, axis=-1)
    shift_c, scale_c, gate_c = jnp.split(jax.nn.silu(c_ctx) @ w_mod + b_mod, 3, axis=-1)
    h = rmsnorm(x, norm_w) * (1 + scale[:, None]) + shift[:, None]
    hc = rmsnorm(xc, norm_w) * (1 + scale_c) + shift_c
    pl = split_cols(h @ w_in, IN_WIDTHS)
    pc = split_cols(hc @ w_in, IN_WIDTHS)
    ya_l, ya_c = mla_branch(pl[0:4], pc[0:4], mla_q_norm, mla_kv_norm, mla_w_q_up, mla_w_kv_up, rope_mla, need_ctx_out)
    yb_l, yb_c = global_gqa_branch(pl[4:8], pc[4:8], gqa_q_norm, gqa_k_norm, rope_head, need_ctx_out)
    yc_l, yc_c = window_gqa_branch(pl[8:12], pc[8:12], swa_sink, rope_head, need_ctx_out)
    yd_l, yd_c = ssd_branch(pl[12:15], pc[12:15], ssd_conv_w, ssd_conv_b, ssd_a_log, ssd_dt_bias, ssd_d,
                            ssd_norm_w, need_ctx_out)
    x = x + gate[:, None] * (jnp.concatenate([ya_l, yb_l, yc_l, yd_l], axis=-1) @ w_out)
    if need_ctx_out:
        xc = xc + gate_c * (jnp.concatenate([ya_c, yb_c, yc_c, yd_c], axis=-1) @ w_out)
    return x, xc


def setup_inputs(seed: int = 0) -> dict:
    key = jax.random.key(seed)
    ks = jax.random.split(key, 24)
    f32 = jnp.float32

    def nrm(k, shape, s):
        return jax.random.normal(k, shape, f32) * s

    dt_init = jnp.exp(jax.random.uniform(ks[18], (DEPTH, 2, SSD_HEADS), f32, jnp.log(1e-3), jnp.log(1e-1)))
    return {
        'x': nrm(ks[0], (BATCH, SEQ, D_MODEL), 1.0),
        'c': nrm(ks[1], (BATCH, D_MODEL), 1.0),
        'ctx': nrm(ks[2], (BATCH, CTX_LEN, D_MODEL), 1.0),
        'c_ctx': nrm(ks[3], (D_MODEL,), 1.0),
        'norm_w': 1.0 + nrm(ks[4], (DEPTH, D_MODEL), 0.02),
        'w_mod': nrm(ks[5], (DEPTH, D_MODEL, 3 * D_MODEL), 0.5 * D_MODEL ** -0.5),
        'b_mod': nrm(ks[6], (DEPTH, 3 * D_MODEL), 0.01),
        'w_in': nrm(ks[7], (DEPTH, D_MODEL, IN_COLS), D_MODEL ** -0.5),
        'mla_q_norm': 1.0 + nrm(ks[8], (DEPTH, MLA_Q_RANK), 0.02),
        'mla_kv_norm': 1.0 + nrm(ks[9], (DEPTH, MLA_KV_RANK), 0.02),
        'mla_w_q_up': nrm(ks[10], (DEPTH, MLA_Q_RANK, MLA_HEADS * (MLA_NOPE + MLA_ROPE)), MLA_Q_RANK ** -0.5),
        'mla_w_kv_up': nrm(ks[11], (DEPTH, MLA_KV_RANK, MLA_HEADS * (MLA_NOPE + MLA_V)), MLA_KV_RANK ** -0.5),
        'gqa_q_norm': 1.0 + nrm(ks[12], (DEPTH, HEAD_DIM), 0.02),
        'gqa_k_norm': 1.0 + nrm(ks[13], (DEPTH, HEAD_DIM), 0.02),
        'swa_sink': nrm(ks[14], (DEPTH, SWA_HEADS), 0.5),
        'ssd_conv_w': nrm(ks[15], (DEPTH, SSD_CONV, SSD_CONV_DIM), SSD_CONV ** -0.5),
        'ssd_conv_b': nrm(ks[16], (DEPTH, SSD_CONV_DIM), 0.01),
        'ssd_a_log': jnp.log(jax.random.uniform(ks[17], (DEPTH, 2, SSD_HEADS), f32, 1.0, 16.0)),
        'ssd_dt_bias': dt_init + jnp.log(-jnp.expm1(-dt_init)),
        'ssd_d': 1.0 + nrm(ks[19], (DEPTH, SSD_HEADS), 0.02),
        'ssd_norm_w': 1.0 + nrm(ks[20], (DEPTH, SSD_INNER), 0.02),
        'w_out': nrm(ks[21], (DEPTH, D_MIX, D_MODEL), D_MIX ** -0.5),
        'final_norm_w': 1.0 + nrm(ks[22], (D_MODEL,), 0.02),
    }


def reference(x, c, ctx, c_ctx, norm_w, w_mod, b_mod, w_in, mla_q_norm, mla_kv_norm, mla_w_q_up, mla_w_kv_up,
              gqa_q_norm, gqa_k_norm, swa_sink, ssd_conv_w, ssd_conv_b, ssd_a_log, ssd_dt_bias, ssd_d,
              ssd_norm_w, w_out, final_norm_w):
    n = x.shape[1]
    rope_head = axial_rope(n, HEAD_DIM)
    rope_mla = axial_rope(n, MLA_ROPE)
    xc = ctx
    for i in range(DEPTH):
        x, xc = hybrid_layer(x, xc, c, c_ctx, norm_w[i], w_mod[i], b_mod[i], w_in[i], mla_q_norm[i], mla_kv_norm[i],
                             mla_w_q_up[i], mla_w_kv_up[i], gqa_q_norm[i], gqa_k_norm[i], swa_sink[i],
                             ssd_conv_w[i], ssd_conv_b[i], ssd_a_log[i], ssd_dt_bias[i], ssd_d[i], ssd_norm_w[i],
                             w_out[i], rope_mla, rope_head, i < DEPTH - 1)
    return rmsnorm(x, final_norm_w)
```

```python
import functools

import numpy as np
import jax
import jax.numpy as jnp
from jax import lax
from jax.experimental import pallas as pl
from jax.experimental.pallas import tpu as pltpu

F32 = jnp.float32
BF16 = jnp.bfloat16

EPS = 1e-6
GRID_W = 64
ROPE_THETA = 10000.0
HEAD_DIM = 64
MLA_HEADS, MLA_NOPE, MLA_ROPE, MLA_V = 4, 64, 32, 64
MLA_Q_RANK, MLA_KV_RANK = 256, 128
GQA_HEADS, GQA_KV_HEADS = 4, 2
SWA_HEADS, SWA_KV_HEADS = 4, 2
WINDOW = 128
SSD_HEADS, SSD_HEAD_DIM, SSD_GROUPS, SSD_STATE, SSD_CONV = 4, 64, 2, 128, 5
SSD_INNER = SSD_HEADS * SSD_HEAD_DIM
SSD_CONV_DIM = SSD_INNER + 2 * SSD_GROUPS * SSD_STATE

LANES = 128
LOG2E = 1.4426950408889634
NEG_BIG = -1e30

C_CQ, C_CKV, C_KR, C_GA = 0, 256, 384, 512
C_QB, C_KB, C_VB, C_GB = 768, 1024, 1152, 1280
C_QC, C_KC, C_VC, C_GC = 1536, 1792, 1920, 2048
C_Z, C_XBC, C_DT, C_END = 2304, 2560, 3328, 3456

TM = 256
TQ = 256
TQ_SWA = 128
SSD_Q = 256
HALO = 8
VMEM_LIMIT = 56 * 1024 * 1024


def _cparams(sem):
    return pltpu.CompilerParams(dimension_semantics=sem, vmem_limit_bytes=VMEM_LIMIT)


def _sigmoid(x):
    return 1.0 / (1.0 + jnp.exp(-x))


def _silu(x):
    return x * _sigmoid(x)


def _rms(x, n):
    return lax.rsqrt(jnp.sum(x * x, axis=-1, keepdims=True) * (1.0 / n) + EPS)


def _dot(a, b):
    return jnp.dot(a, b, preferred_element_type=F32)


def _dot_nt(a, b):
    return lax.dot_general(a, b, (((1,), (1,)), ((), ())), preferred_element_type=F32)


def _dot_tn(a, b):
    return lax.dot_general(a, b, (((0,), (0,)), ((), ())), preferred_element_type=F32)


def _lane_iota(rows):
    return lax.broadcasted_iota(jnp.int32, (rows, LANES), 1)


def _mod_kernel(c_ref, w_ref, b_ref, o_ref):
    cs = _silu(c_ref[...])
    o_ref[0] = _dot(cs.astype(BF16), w_ref[0].astype(BF16)) + b_ref[0]


def _modulation(cvec, w_mod, b_mod):
    depth, d, d3 = w_mod.shape
    tn = 512
    return pl.pallas_call(
        _mod_kernel,
        grid=(depth, d3 // tn),
        in_specs=[pl.BlockSpec((8, d), lambda l, j: (0, 0)),
                  pl.BlockSpec((1, d, tn), lambda l, j: (l, 0, j)),
                  pl.BlockSpec((1, 1, tn), lambda l, j: (l, 0, j))],
        out_specs=pl.BlockSpec((1, 8, tn), lambda l, j: (l, 0, j)),
        out_shape=jax.ShapeDtypeStruct((depth, 8, d3), F32),
        compiler_params=_cparams(("parallel", "parallel")),
        name="modulation",
    )(cvec, w_mod, b_mod.reshape(depth, 1, d3))


def _rope(x, c, s_up, s_dn, half):
    return x * c + pltpu.roll(x, LANES - half, 1) * s_up + pltpu.roll(x, half, 1) * s_dn


def _head_rms(blk, lane):
    sq = blk * blk
    lo = lane < HEAD_DIM
    s_lo = jnp.sum(jnp.where(lo, sq, 0.0), axis=-1, keepdims=True)
    s_hi = jnp.sum(jnp.where(lo, 0.0, sq), axis=-1, keepdims=True)
    r_lo = lax.rsqrt(s_lo * (1.0 / HEAD_DIM) + EPS)
    r_hi = lax.rsqrt(s_hi * (1.0 / HEAD_DIM) + EPS)
    return jnp.where(lo, r_lo, r_hi)


def _dup_halves(blk, lane):
    sw = pltpu.roll(blk, HEAD_DIM, 1)
    lo = lane < HEAD_DIM
    return jnp.where(lo, blk, sw), jnp.where(lo, sw, blk)


def _inproj_kernel(x_ref, mod_ref, nw_ref, wcat_ref, wq_ref, wkv_ref, pv_ref, rope_ref,
                   qall_ref, kall_ref, vall_ref, swa_ref, g_ref, xbc_ref, dt_ref):
    tm = x_ref.shape[1]
    d = x_ref.shape[2]
    lane = _lane_iota(tm)
    lo = lane < HEAD_DIM

    x = x_ref[0]
    xn = x * _rms(x, d)
    h = (xn * nw_ref[...]) * (1.0 + mod_ref[0, 1:2, :]) + mod_ref[0, 0:1, :]
    p = _dot(h.astype(BF16), wcat_ref[...])

    ca, sa_up, sa_dn = rope_ref[0], rope_ref[1], rope_ref[2]
    cb, sb_up, sb_dn = rope_ref[3], rope_ref[4], rope_ref[5]
    qscale_a = (MLA_NOPE + MLA_ROPE) ** -0.5 * LOG2E
    qscale_h = HEAD_DIM ** -0.5 * LOG2E

    cq = p[:, C_CQ:C_CQ + MLA_Q_RANK]
    cqn = cq * _rms(cq, MLA_Q_RANK) * pv_ref[0:1, :]
    qa = _dot(cqn.astype(BF16), wq_ref[...])
    ckv = p[:, C_CKV:C_CKV + MLA_KV_RANK]
    ckvn = ckv * _rms(ckv, MLA_KV_RANK) * pv_ref[1:2, 0:MLA_KV_RANK]
    kv = _dot(ckvn.astype(BF16), wkv_ref[...])
    kr = _rope(p[:, C_KR:C_KR + LANES], ca, sa_up, sa_dn, MLA_ROPE // 2)
    for hh in range(MLA_HEADS):
        sl = slice(hh * LANES, (hh + 1) * LANES)
        qall_ref[0, :, sl] = (_rope(qa[:, sl], ca, sa_up, sa_dn, MLA_ROPE // 2) * qscale_a).astype(BF16)
        kall_ref[0, :, sl] = (kv[:, sl] + kr).astype(BF16)
    vall_ref[0, :, 0:2 * LANES] = kv[:, 4 * LANES:6 * LANES].astype(BF16)

    base = MLA_HEADS * LANES
    for blk_i in range(2):
        qb = p[:, C_QB + blk_i * LANES:C_QB + (blk_i + 1) * LANES]
        qb = qb * _head_rms(qb, lane) * pv_ref[2:3, 0:LANES]
        qb = _rope(qb, cb, sb_up, sb_dn, HEAD_DIM // 2) * qscale_h
        qall_ref[0, :, base + (2 * blk_i) * LANES:base + (2 * blk_i + 1) * LANES] = jnp.where(lo, qb, 0.0).astype(BF16)
        qall_ref[0, :, base + (2 * blk_i + 1) * LANES:base + (2 * blk_i + 2) * LANES] = jnp.where(lo, 0.0, qb).astype(BF16)
    kb = p[:, C_KB:C_KB + LANES]
    kb = kb * _head_rms(kb, lane) * pv_ref[3:4, 0:LANES]
    kb = _rope(kb, cb, sb_up, sb_dn, HEAD_DIM // 2)
    k0, k1 = _dup_halves(kb, lane)
    k0 = k0.astype(BF16)
    k1 = k1.astype(BF16)
    kall_ref[0, :, base:base + LANES] = k0
    kall_ref[0, :, base + LANES:base + 2 * LANES] = k0
    kall_ref[0, :, base + 2 * LANES:base + 3 * LANES] = k1
    kall_ref[0, :, base + 3 * LANES:base + 4 * LANES] = k1
    v0, v1 = _dup_halves(p[:, C_VB:C_VB + LANES], lane)
    vall_ref[0, :, 2 * LANES:3 * LANES] = v0.astype(BF16)
    vall_ref[0, :, 3 * LANES:4 * LANES] = v1.astype(BF16)

    for blk_i in range(2):
        qc = p[:, C_QC + blk_i * LANES:C_QC + (blk_i + 1) * LANES]
        swa_ref[0, :, blk_i * LANES:(blk_i + 1) * LANES] = (
            _rope(qc, cb, sb_up, sb_dn, HEAD_DIM // 2) * qscale_h).astype(BF16)
    kc = _rope(p[:, C_KC:C_KC + LANES], cb, sb_up, sb_dn, HEAD_DIM // 2)
    kc0, kc1 = _dup_halves(kc, lane)
    swa_ref[0, :, 2 * LANES:3 * LANES] = kc0.astype(BF16)
    swa_ref[0, :, 3 * LANES:4 * LANES] = kc1.astype(BF16)
    vc0, vc1 = _dup_halves(p[:, C_VC:C_VC + LANES], lane)
    swa_ref[0, :, 4 * LANES:5 * LANES] = vc0.astype(BF16)
    swa_ref[0, :, 5 * LANES:6 * LANES] = vc1.astype(BF16)

    g_ref[0, :, 0:256] = _silu(p[:, C_GA:C_GA + 256])
    g_ref[0, :, 256:512] = _silu(p[:, C_GB:C_GB + 256])
    g_ref[0, :, 512:768] = _silu(p[:, C_GC:C_GC + 256])
    g_ref[0, :, 768:1024] = _silu(p[:, C_Z:C_Z + 256])
    xbc_ref[0] = p[:, C_XBC:C_XBC + SSD_CONV_DIM]
    dtr = p[:, C_DT:C_DT + LANES] + pv_ref[4:5, 0:LANES]
    dt_ref[0] = jnp.maximum(dtr, 0.0) + jnp.log(1.0 + jnp.exp(-jnp.abs(dtr)))


def _inproj(xall, mods, nw, wcat, wq, wkv, pvec, ropes, n_ctx):
    bsz, t, d = xall.shape
    nct = n_ctx // TM
    grid = (bsz, t // TM)
    tok = lambda w: pl.BlockSpec((1, TM, w), lambda b, i: (b, i, 0))
    full = lambda a: pl.BlockSpec(a.shape, lambda b, i: (0,) * a.ndim)
    widths = (8 * LANES, 8 * LANES, 4 * LANES, 6 * LANES, 1024, SSD_CONV_DIM, LANES)
    dtypes = (BF16, BF16, BF16, BF16, F32, F32, F32)
    return pl.pallas_call(
        _inproj_kernel,
        grid=grid,
        in_specs=[tok(d),
                  pl.BlockSpec((1, 3, d), lambda b, i: (jnp.where(i < nct, bsz, b), 0, 0)),
                  full(nw), full(wcat), full(wq), full(wkv), full(pvec),
                  pl.BlockSpec((6, TM, LANES), lambda b, i: (0, i, 0))],
        out_specs=[tok(w) for w in widths],
        out_shape=[jax.ShapeDtypeStruct((bsz, t, w), dt) for w, dt in zip(widths, dtypes)],
        compiler_params=_cparams(("parallel", "parallel")),
        name="inproj",
    )(xall, mods, nw, wcat, wq, wkv, pvec, ropes)


def _softmax_pv(q, k, v):
    s = _dot_nt(q, k)
    m = jnp.max(s, axis=-1, keepdims=True)
    pr = jnp.exp2(s - m)
    l = jnp.sum(pr, axis=-1, keepdims=True)
    return _dot(pr.astype(BF16), v) / l


def _attn_kernel(q_ref, k_ref, v_ref, o_ref, *, n_ctx, ctx_blocks):
    tq = q_ref.shape[1]
    lane = _lane_iota(tq)

    def run(nk):
        outs = []
        for j in range(2):
            sl = slice(j * LANES, (j + 1) * LANES)
            outs.append(_softmax_pv(q_ref[0, :, sl], k_ref[0, 0:nk, sl], v_ref[0, 0:nk, :]))
        o_ref[0] = jnp.where(lane < HEAD_DIM, outs[0], outs[1])

    if ctx_blocks == 0:
        run(k_ref.shape[1])
    else:
        i = pl.program_id(2)
        pl.when(i < ctx_blocks)(lambda: run(n_ctx))
        pl.when(i >= ctx_blocks)(lambda: run(k_ref.shape[1]))


def _attention(qall, kall, vall, n_ctx, with_ctx):
    bsz, t, _ = qall.shape
    ctx_blocks = n_ctx // TQ
    off = 0 if with_ctx else ctx_blocks
    grid = (bsz, 4, t // TQ - off)
    return pl.pallas_call(
        functools.partial(_attn_kernel, n_ctx=n_ctx, ctx_blocks=ctx_blocks if with_ctx else 0),
        grid=grid,
        in_specs=[pl.BlockSpec((1, TQ, 2 * LANES), lambda b, p, i: (b, i + off, p)),
                  pl.BlockSpec((1, t, 2 * LANES), lambda b, p, i: (b, 0, p)),
                  pl.BlockSpec((1, t, LANES), lambda b, p, i: (b, 0, p))],
        out_specs=pl.BlockSpec((1, TQ, LANES), lambda b, p, i: (b, i + off, p)),
        out_shape=jax.ShapeDtypeStruct((bsz, t, 4 * LANES), F32),
        compiler_params=_cparams(("parallel", "parallel", "arbitrary")),
        name="attn_global",
    )(qall, kall, vall)


def _swa_kernel(q_ref, k_ref, v_ref, sink_ref, o_ref, *, n_ctx):
    tq = q_ref.shape[1]
    t = k_ref.shape[1]
    band = 3 * TQ_SWA
    i = pl.program_id(2)
    pair = pl.program_id(1)
    lane = _lane_iota(tq)
    lo = lane < HEAD_DIM

    start = jnp.clip((i - 1) * TQ_SWA, 0, t - band)
    start = pl.multiple_of(start, TQ_SWA)
    q_pos = i * TQ_SWA - n_ctx + lax.broadcasted_iota(jnp.int32, (tq, band), 0)
    k_pos = start - n_ctx + lax.broadcasted_iota(jnp.int32, (tq, band), 1)
    valid = (jnp.abs(k_pos - q_pos) <= WINDOW) & (k_pos >= 0) & (q_pos >= 0)

    kc = k_ref[0, 0:n_ctx, :]
    vc = v_ref[0, 0:n_ctx, :]
    kl = k_ref[0, pl.ds(start, band), :]
    vl = v_ref[0, pl.ds(start, band), :]
    q = q_ref[0]
    outs = []
    for j in range(2):
        qj = jnp.where(lo, q, jnp.zeros_like(q)) if j == 0 else jnp.where(lo, jnp.zeros_like(q), q)
        s_ctx = _dot_nt(qj, kc)
        s_loc = jnp.where(valid, _dot_nt(qj, kl), NEG_BIG)
        sink = jnp.where(pair == 0, sink_ref[j:j + 1, 0:1], sink_ref[2 + j:3 + j, 0:1]) * LOG2E
        m = jnp.maximum(jnp.maximum(jnp.max(s_ctx, axis=-1, keepdims=True),
                                    jnp.max(s_loc, axis=-1, keepdims=True)), sink)
        p_ctx = jnp.exp2(s_ctx - m)
        p_loc = jnp.exp2(s_loc - m)
        l = (jnp.sum(p_ctx, axis=-1, keepdims=True) + jnp.sum(p_loc, axis=-1, keepdims=True)
             + jnp.exp2(sink - m))
        outs.append((_dot(p_ctx.astype(BF16), vc) + _dot(p_loc.astype(BF16), vl)) / l)
    o_ref[0] = jnp.where(lo, outs[0], outs[1])


def _swa(swa_in, sink_rows, n_ctx):
    bsz, t, _ = swa_in.shape
    grid = (bsz, 2, t // TQ_SWA)
    return pl.pallas_call(
        functools.partial(_swa_kernel, n_ctx=n_ctx),
        grid=grid,
        in_specs=[pl.BlockSpec((1, TQ_SWA, LANES), lambda b, p, i: (b, i, p)),
                  pl.BlockSpec((1, t, LANES), lambda b, p, i: (b, 0, 2 + p)),
                  pl.BlockSpec((1, t, LANES), lambda b, p, i: (b, 0, 4 + p)),
                  pl.BlockSpec(sink_rows.shape, lambda b, p, i: (0, 0))],
        out_specs=pl.BlockSpec((1, TQ_SWA, LANES), lambda b, p, i: (b, i, p)),
        out_shape=jax.ShapeDtypeStruct((bsz, t, 2 * LANES), F32),
        compiler_params=_cparams(("parallel", "parallel", "arbitrary")),
        name="attn_window",
    )(swa_in, swa_in, swa_in, sink_rows)


def _conv_kernel(x_ref, prev_ref, next_ref, w_ref, b_ref, u_ref, ext_ref, *, ctx_tiles):
    tm = x_ref.shape[1]
    i = pl.program_id(1)
    first = (i == 0) | (i == ctx_tiles)
    last = (i == ctx_tiles - 1) | (i == pl.num_programs(1) - 1)
    ext_ref[0:HALO, :] = jnp.where(first, 0.0, prev_ref[0])
    ext_ref[HALO:HALO + tm, :] = x_ref[0]
    ext_ref[HALO + tm:2 * HALO + tm, :] = jnp.where(last, 0.0, next_ref[0])
    acc = jnp.zeros((tm, x_ref.shape[2]), F32) + b_ref[...]
    for tap in range(SSD_CONV):
        acc = acc + ext_ref[pl.ds(HALO - SSD_CONV // 2 + tap, tm), :] * w_ref[tap:tap + 1, :]
    u_ref[0] = _silu(acc)


def _conv(xbc, conv_w8, conv_b, n_ctx):
    bsz, t, c = xbc.shape
    per = TM // HALO
    nh = t // HALO
    return pl.pallas_call(
        functools.partial(_conv_kernel, ctx_tiles=n_ctx // TM),
        grid=(bsz, t // TM),
        in_specs=[pl.BlockSpec((1, TM, c), lambda b, i: (b, i, 0)),
                  pl.BlockSpec((1, HALO, c), lambda b, i: (b, jnp.maximum(i * per - 1, 0), 0)),
                  pl.BlockSpec((1, HALO, c), lambda b, i: (b, jnp.minimum((i + 1) * per, nh - 1), 0)),
                  pl.BlockSpec(conv_w8.shape, lambda b, i: (0, 0)),
                  pl.BlockSpec(conv_b.shape, lambda b, i: (0, 0))],
        out_specs=pl.BlockSpec((1, TM, c), lambda b, i: (b, i, 0)),
        out_shape=jax.ShapeDtypeStruct((bsz, t, c), F32),
        scratch_shapes=[pltpu.VMEM((TM + 2 * HALO, c), F32)],
        compiler_params=_cparams(("parallel", "parallel")),
        name="ssd_conv",
    )(xbc, xbc, xbc, conv_w8, conv_b)


def _ssd_kernel(u_ref, dt_ref, alog_ref, y_ref, st_ref):
    q = u_ref.shape[1]
    direction = pl.program_id(1)
    step = pl.program_id(2)
    fwd = direction == 0

    @pl.when(step == 0)
    def _():
        st_ref[...] = jnp.zeros_like(st_ref)

    lane1 = lax.broadcasted_iota(jnp.int32, (1, LANES), 1)
    avec = jnp.where(lane1 < 2 * SSD_HEADS, -jnp.exp(alog_ref[...]), 0.0)
    dt = dt_ref[0]
    dta = dt * avec

    row = lax.broadcasted_iota(jnp.int32, (q, q), 0)
    col = lax.broadcasted_iota(jnp.int32, (q, q), 1)
    sgn = 1 - 2 * direction
    mask = (row - col) * sgn >= 0
    ones = mask.astype(BF16)
    hi = dta.astype(BF16)
    r1 = dta - hi.astype(F32)
    mid = r1.astype(BF16)
    lo3 = (r1 - mid.astype(F32)).astype(BF16)
    a_cum = _dot(ones, hi) + _dot(ones, mid) + _dot(ones, lo3)
    a_cum_t = a_cum.T
    a_tot = jnp.where(fwd, a_cum[q - 1:q, :], a_cum[0:1, :])

    def col_of(arr, hh):
        return jnp.where(fwd, arr[:, hh:hh + 1], arr[:, SSD_HEADS + hh:SSD_HEADS + hh + 1])

    def row_of(arr, hh):
        return jnp.where(fwd, arr[hh:hh + 1, :], arr[SSD_HEADS + hh:SSD_HEADS + hh + 1, :])

    laneq = _lane_iota(q)
    lo = laneq < SSD_HEAD_DIM
    lo1 = lane1 < SSD_HEAD_DIM
    for g in range(SSD_GROUPS):
        h0, h1 = 2 * g, 2 * g + 1
        xs = u_ref[0, :, g * LANES:(g + 1) * LANES]
        bg = u_ref[0, :, SSD_INNER + g * SSD_STATE:SSD_INNER + (g + 1) * SSD_STATE].astype(BF16)
        cg = u_ref[0, :, SSD_INNER + (SSD_GROUPS + g) * SSD_STATE:
                   SSD_INNER + (SSD_GROUPS + g + 1) * SSD_STATE].astype(BF16)
        gmat = _dot_nt(cg, bg)
        acl = jnp.where(lo, col_of(a_cum, h0), col_of(a_cum, h1))
        dtl = jnp.where(lo, col_of(dt, h0), col_of(dt, h1))
        atl = jnp.where(lo1, col_of(a_tot, h0), col_of(a_tot, h1))
        xdt = xs * dtl
        xdt_b = xdt.astype(BF16)
        ys = []
        for hh in (h0, h1):
            seg = col_of(a_cum, hh) - row_of(a_cum_t, hh)
            dec = jnp.exp(jnp.where(mask, seg, NEG_BIG))
            ys.append(_dot((gmat * dec).astype(BF16), xdt_b))
        st = st_ref[g]
        y = jnp.where(lo, ys[0], ys[1]) + _dot(cg, st.astype(BF16)) * jnp.exp(acl)
        y_ref[0, 0, :, g * LANES:(g + 1) * LANES] = y
        w = (xdt * jnp.exp(atl - acl)).astype(BF16)
        st_ref[g] = st * jnp.exp(atl) + _dot_tn(bg, w)


def _ssd(u, dt, alog_row, n_ctx):
    bsz, t, c = u.shape
    nc = t // SSD_Q
    ncc = n_ctx // SSD_Q

    def chunk(d, s):
        back = jnp.where(s < ncc, ncc - 1 - s, nc - 1 - s + ncc)
        return jnp.where(d == 0, s, back)

    return pl.pallas_call(
        _ssd_kernel,
        grid=(bsz, 2, nc),
        in_specs=[pl.BlockSpec((1, SSD_Q, c), lambda b, d, s: (b, chunk(d, s), 0)),
                  pl.BlockSpec((1, SSD_Q, LANES), lambda b, d, s: (b, chunk(d, s), 0)),
                  pl.BlockSpec((1, LANES), lambda b, d, s: (0, 0))],
        out_specs=pl.BlockSpec((1, 1, SSD_Q, SSD_INNER), lambda b, d, s: (b, d, chunk(d, s), 0)),
        out_shape=jax.ShapeDtypeStruct((bsz, 2, t, SSD_INNER), F32),
        scratch_shapes=[pltpu.VMEM((SSD_GROUPS, SSD_STATE, LANES), F32)],
        compiler_params=_cparams(("parallel", "parallel", "arbitrary")),
        name="ssd_scan",
    )(u, dt, alog_row)


def _outproj_kernel(oab_ref, oc_ref, y_ref, u_ref, g_ref, x_ref, mod_ref, pv_ref, w_ref, fw_ref, o_ref, *, final):
    g = g_ref[0]
    oab = oab_ref[0]
    m_a = oab[:, 0:256] * g[:, 0:256]
    m_b = oab[:, 256:512] * g[:, 256:512]
    m_c = oc_ref[0] * g[:, 512:768]
    yd = pv_ref[0:1, :] * u_ref[0] + y_ref[0, 0] + y_ref[0, 1]
    td = yd * g[:, 768:1024]
    m_d = td * _rms(td, SSD_INNER) * pv_ref[1:2, :]
    acc = _dot(m_a.astype(BF16), w_ref[0:256, :])
    acc += _dot(m_b.astype(BF16), w_ref[256:512, :])
    acc += _dot(m_c.astype(BF16), w_ref[512:768, :])
    acc += _dot(m_d.astype(BF16), w_ref[768:1024, :])
    xn = x_ref[0] + mod_ref[0, 2:3, :] * acc
    if final:
        xn = xn * _rms(xn, xn.shape[-1]) * fw_ref[...]
    o_ref[0] = xn


def _outproj(oab, oc, y, u, gates, xall, mods, pvec, wout, fw, n_ctx, final):
    bsz, t, d = xall.shape
    nct = n_ctx // TM
    off = nct if final else 0
    rows = t - off * TM
    tok = lambda w: pl.BlockSpec((1, TM, w), lambda b, i: (b, i + off, 0))
    full = lambda a: pl.BlockSpec(a.shape, lambda b, i: (0,) * a.ndim)
    return pl.pallas_call(
        functools.partial(_outproj_kernel, final=final),
        grid=(bsz, rows // TM),
        in_specs=[tok(4 * LANES), tok(2 * LANES),
                  pl.BlockSpec((1, 2, TM, SSD_INNER), lambda b, i: (b, 0, i + off, 0)),
                  tok(SSD_INNER), tok(1024), tok(d),
                  pl.BlockSpec((1, 3, d), lambda b, i: (jnp.where(i + off < nct, bsz, b), 0, 0)),
                  full(pvec), full(wout), full(fw)],
        out_specs=pl.BlockSpec((1, TM, d), lambda b, i: (b, i, 0)),
        out_shape=jax.ShapeDtypeStruct((bsz, rows, d), F32),
        compiler_params=_cparams(("parallel", "parallel")),
        name="outproj",
    )(oab, oc, y, u, gates, xall, mods, pvec, wout, fw)


def _rope_tables(n_ctx, n_lat):
    rows = n_lat // GRID_W
    row = jnp.repeat(jnp.arange(rows, dtype=F32), GRID_W)
    col = jnp.tile(jnp.arange(GRID_W, dtype=F32), rows)

    def cos_sin(rot_dim):
        n_freq = rot_dim // 4
        inv_freq = ROPE_THETA ** (-jnp.arange(n_freq, dtype=F32) / n_freq)
        ang = jnp.concatenate([row[:, None] * inv_freq, col[:, None] * inv_freq], axis=-1)
        return jnp.cos(ang), jnp.sin(ang)

    def with_ctx(tab, fill):
        return jnp.concatenate([jnp.full((n_ctx, LANES), fill, F32), tab], axis=0)

    lane = np.arange(LANES)
    cos_a, sin_a = cos_sin(MLA_ROPE)
    half = MLA_ROPE // 2
    idx = np.where((lane >= MLA_NOPE) & (lane < MLA_NOPE + MLA_ROPE), (lane - MLA_NOPE) % half, 0)
    first = (lane >= MLA_NOPE) & (lane < MLA_NOPE + half)
    second = (lane >= MLA_NOPE + half) & (lane < MLA_NOPE + MLA_ROPE)
    rot = first | second
    ca = jnp.where(rot, cos_a[:, idx], 1.0)
    sa_up = jnp.where(first, -sin_a[:, idx], 0.0)
    sa_dn = jnp.where(second, sin_a[:, idx], 0.0)
    cos_h, sin_h = cos_sin(HEAD_DIM)
    half = HEAD_DIM // 2
    idx = lane % half
    first = (lane % HEAD_DIM) < half
    cb = cos_h[:, idx]
    sb_up = jnp.where(first, -sin_h[:, idx], 0.0)
    sb_dn = jnp.where(first, 0.0, sin_h[:, idx])
    return jnp.stack([with_ctx(ca, 1.0), with_ctx(sa_up, 0.0), with_ctx(sa_dn, 0.0),
                      with_ctx(cb, 1.0), with_ctx(sb_up, 0.0), with_ctx(sb_dn, 0.0)])


def _relayout_w_in(w_in):
    depth, d, _ = w_in.shape
    widths = (MLA_Q_RANK, MLA_KV_RANK, MLA_ROPE, MLA_HEADS * MLA_V,
              256, 128, 128, 256, 256, 128, 128, 256,
              SSD_INNER, SSD_CONV_DIM, 2 * SSD_HEADS)
    parts = jnp.split(w_in, [int(v) for v in np.cumsum(widths)[:-1]], axis=-1)
    z = lambda n: jnp.zeros((depth, d, n), w_in.dtype)
    kr = jnp.concatenate([z(MLA_NOPE), parts[2], z(LANES - MLA_NOPE - MLA_ROPE)], axis=-1)
    dtp = jnp.concatenate([parts[14], z(LANES - 2 * SSD_HEADS)], axis=-1)
    cat = jnp.concatenate(parts[0:2] + [kr] + parts[3:14] + [dtp], axis=-1)
    assert cat.shape[-1] == C_END
    return cat.astype(BF16)


def _relayout_mla(w_q_up, w_kv_up):
    depth = w_q_up.shape[0]
    wq = w_q_up.reshape(depth, MLA_Q_RANK, MLA_HEADS, MLA_NOPE + MLA_ROPE)
    wq = jnp.pad(wq, ((0, 0), (0, 0), (0, 0), (0, LANES - MLA_NOPE - MLA_ROPE)))
    wq = wq.reshape(depth, MLA_Q_RANK, MLA_HEADS * LANES)
    wkv = w_kv_up.reshape(depth, MLA_KV_RANK, MLA_HEADS, MLA_NOPE + MLA_V)
    wk = jnp.pad(wkv[..., :MLA_NOPE], ((0, 0), (0, 0), (0, 0), (0, LANES - MLA_NOPE)))
    wk = wk.reshape(depth, MLA_KV_RANK, MLA_HEADS * LANES)
    wv = wkv[..., MLA_NOPE:].reshape(depth, MLA_KV_RANK, MLA_HEADS * MLA_V)
    return wq.astype(BF16), jnp.concatenate([wk, wv], axis=-1).astype(BF16)


def _pad_lanes(v, width):
    return jnp.pad(v, ((0, 0), (0, width - v.shape[-1])))


def kernel(x, c, ctx, c_ctx, norm_w, w_mod, b_mod, w_in, mla_q_norm, mla_kv_norm, mla_w_q_up, mla_w_kv_up,
           gqa_q_norm, gqa_k_norm, swa_sink, ssd_conv_w, ssd_conv_b, ssd_a_log, ssd_dt_bias, ssd_d,
           ssd_norm_w, w_out, final_norm_w):
    bsz, n_lat, d = x.shape
    n_ctx = ctx.shape[1]
    depth = w_in.shape[0]
    assert d == 1024 and bsz + 1 <= 8
    assert n_ctx % TM == 0 and n_ctx % TQ == 0 and n_ctx % SSD_Q == 0 and n_lat % TM == 0
    assert n_lat % GRID_W == 0 and n_lat >= 3 * TQ_SWA

    xall = jnp.concatenate([ctx, x], axis=1)
    cvec = jnp.concatenate([c, c_ctx[None, :], jnp.zeros((8 - bsz - 1, d), F32)], axis=0)
    mods = _modulation(cvec, w_mod, b_mod).reshape(depth, 8, 3, d)[:, :bsz + 1]

    wcat = _relayout_w_in(w_in)
    wq, wkv = _relayout_mla(mla_w_q_up, mla_w_kv_up)
    wout = w_out.astype(BF16)
    ropes = _rope_tables(n_ctx, n_lat)
    pv_in = jnp.stack([mla_q_norm,
                       _pad_lanes(mla_kv_norm, 256),
                       jnp.tile(gqa_q_norm, (1, 4)),
                       _pad_lanes(jnp.tile(gqa_k_norm, (1, 2)), 256),
                       _pad_lanes(ssd_dt_bias.reshape(depth, -1), 256)], axis=1)
    pv_in = jnp.pad(pv_in, ((0, 0), (0, 3), (0, 0)))
    pv_out = jnp.stack([jnp.repeat(ssd_d, SSD_HEAD_DIM, axis=-1), ssd_norm_w], axis=1)
    pv_out = jnp.pad(pv_out, ((0, 0), (0, 6), (0, 0)))
    sink_rows = jnp.broadcast_to(jnp.pad(swa_sink, ((0, 0), (0, 4)))[:, :, None], (depth, 8, LANES))
    conv_w8 = jnp.pad(ssd_conv_w, ((0, 0), (0, 8 - SSD_CONV), (0, 0)))
    alog = _pad_lanes(ssd_a_log.reshape(depth, -1), LANES)
    fw = final_norm_w.reshape(1, d)

    for l in range(depth):
        final = l == depth - 1
        qall, kall, vall, swa_in, gates, xbc, dt = _inproj(
            xall, mods[l], norm_w[l].reshape(1, d), wcat[l], wq[l], wkv[l], pv_in[l], ropes, n_ctx)
        oab = _attention(qall, kall, vall, n_ctx, with_ctx=not final)
        oc = _swa(swa_in, sink_rows[l], n_ctx)
        u = _conv(xbc, conv_w8[l], ssd_conv_b[l].reshape(1, -1), n_ctx)
        y = _ssd(u, dt, alog[l].reshape(1, LANES), n_ctx)
        xall = _outproj(oab, oc, y, u, gates, xall, mods[l], pv_out[l], wout[l], fw, n_ctx, final)
    return xall
```

```python
import functools

import numpy as np
import jax
import jax.numpy as jnp
from jax import lax
from jax.experimental import pallas as pl
from jax.experimental.pallas import tpu as pltpu

F32 = jnp.float32
BF16 = jnp.bfloat16

EPS = 1e-6
GRID_W = 64
ROPE_THETA = 10000.0
HEAD_DIM = 64
MLA_HEADS, MLA_NOPE, MLA_ROPE, MLA_V = 4, 64, 32, 64
MLA_Q_RANK, MLA_KV_RANK = 256, 128
GQA_HEADS, GQA_KV_HEADS = 4, 2
SWA_HEADS, SWA_KV_HEADS = 4, 2
WINDOW = 128
SSD_HEADS, SSD_HEAD_DIM, SSD_GROUPS, SSD_STATE, SSD_CONV = 4, 64, 2, 128, 5
SSD_INNER = SSD_HEADS * SSD_HEAD_DIM
SSD_CONV_DIM = SSD_INNER + 2 * SSD_GROUPS * SSD_STATE

LANES = 128
LOG2E = 1.4426950408889634
NEG_BIG = -1e30

C_CQ, C_CKV, C_KR, C_GA = 0, 256, 384, 512
C_QB, C_KB, C_VB, C_GB = 768, 1024, 1152, 1280
C_QC, C_KC, C_VC, C_GC = 1536, 1792, 1920, 2048
C_Z, C_XBC, C_DT, C_END = 2304, 2560, 3328, 3456

TM = 256
TQ = 512
TQ_CTX = 256
KV_CHUNK = 1024
TQ_SWA = 256
SSD_Q = 256
HALO = 8
VMEM_LIMIT = 56 * 1024 * 1024


def _cparams(sem):
    return pltpu.CompilerParams(dimension_semantics=sem, vmem_limit_bytes=VMEM_LIMIT)


def _sigmoid(x):
    return 1.0 / (1.0 + jnp.exp(-x))


def _silu(x):
    return x * _sigmoid(x)


def _rms(x, n):
    return lax.rsqrt(jnp.sum(x * x, axis=-1, keepdims=True) * (1.0 / n) + EPS)


def _dot(a, b):
    return jnp.dot(a, b, preferred_element_type=F32)


def _dot_nt(a, b):
    return lax.dot_general(a, b, (((1,), (1,)), ((), ())), preferred_element_type=F32)


def _dot_tn(a, b):
    return lax.dot_general(a, b, (((0,), (0,)), ((), ())), preferred_element_type=F32)


def _lane_iota(rows):
    return lax.broadcasted_iota(jnp.int32, (rows, LANES), 1)


def _mod_kernel(c_ref, w_ref, b_ref, o_ref):
    cs = _silu(c_ref[...])
    o_ref[0] = _dot(cs.astype(BF16), w_ref[0].astype(BF16)) + b_ref[0]


def _modulation(cvec, w_mod, b_mod):
    depth, d, d3 = w_mod.shape
    tn = 512
    return pl.pallas_call(
        _mod_kernel,
        grid=(depth, d3 // tn),
        in_specs=[pl.BlockSpec((8, d), lambda l, j: (0, 0)),
                  pl.BlockSpec((1, d, tn), lambda l, j: (l, 0, j)),
                  pl.BlockSpec((1, 1, tn), lambda l, j: (l, 0, j))],
        out_specs=pl.BlockSpec((1, 8, tn), lambda l, j: (l, 0, j)),
        out_shape=jax.ShapeDtypeStruct((depth, 8, d3), F32),
        compiler_params=_cparams(("parallel", "parallel")),
        name="modulation",
    )(cvec, w_mod, b_mod.reshape(depth, 1, d3))


def _rope(x, c, s_up, s_dn, half):
    return x * c + pltpu.roll(x, LANES - half, 1) * s_up + pltpu.roll(x, half, 1) * s_dn


def _head_rms(blk, lane):
    sq = blk * blk
    lo = lane < HEAD_DIM
    s_lo = jnp.sum(jnp.where(lo, sq, 0.0), axis=-1, keepdims=True)
    s_hi = jnp.sum(jnp.where(lo, 0.0, sq), axis=-1, keepdims=True)
    r_lo = lax.rsqrt(s_lo * (1.0 / HEAD_DIM) + EPS)
    r_hi = lax.rsqrt(s_hi * (1.0 / HEAD_DIM) + EPS)
    return jnp.where(lo, r_lo, r_hi)


def _dup_halves(blk, lane):
    sw = pltpu.roll(blk, HEAD_DIM, 1)
    lo = lane < HEAD_DIM
    return jnp.where(lo, blk, sw), jnp.where(lo, sw, blk)


def _store_v_slots(vall_ref, slot, pair, lo):
    vall_ref[0, :, slot * LANES:(slot + 1) * LANES] = jnp.where(lo, pair, 1.0).astype(BF16)
    vall_ref[0, :, (slot + 1) * LANES:(slot + 2) * LANES] = jnp.where(lo, 1.0, pair).astype(BF16)


def _inproj_kernel(x_ref, mod_ref, nw_ref, wcat_ref, wq_ref, wkv_ref, pv_ref, rope_ref,
                   qall_ref, kall_ref, vall_ref, swa_ref, g_ref, xbc_ref, dt_ref):
    tm = x_ref.shape[1]
    d = x_ref.shape[2]
    lane = _lane_iota(tm)
    lo = lane < HEAD_DIM

    x = x_ref[0]
    xn = x * _rms(x, d)
    h = (xn * nw_ref[...]) * (1.0 + mod_ref[0, 1:2, :]) + mod_ref[0, 0:1, :]
    p = _dot(h.astype(BF16), wcat_ref[...])

    ca, sa_up, sa_dn = rope_ref[0], rope_ref[1], rope_ref[2]
    cb, sb_up, sb_dn = rope_ref[3], rope_ref[4], rope_ref[5]
    qscale_a = (MLA_NOPE + MLA_ROPE) ** -0.5 * LOG2E
    qscale_h = HEAD_DIM ** -0.5 * LOG2E

    cq = p[:, C_CQ:C_CQ + MLA_Q_RANK]
    cqn = cq * _rms(cq, MLA_Q_RANK) * pv_ref[0:1, :]
    qa = _dot(cqn.astype(BF16), wq_ref[...])
    ckv = p[:, C_CKV:C_CKV + MLA_KV_RANK]
    ckvn = ckv * _rms(ckv, MLA_KV_RANK) * pv_ref[1:2, 0:MLA_KV_RANK]
    kv = _dot(ckvn.astype(BF16), wkv_ref[...])
    kr = _rope(p[:, C_KR:C_KR + LANES], ca, sa_up, sa_dn, MLA_ROPE // 2)
    for hh in range(MLA_HEADS):
        sl = slice(hh * LANES, (hh + 1) * LANES)
        qall_ref[0, :, sl] = (_rope(qa[:, sl], ca, sa_up, sa_dn, MLA_ROPE // 2) * qscale_a).astype(BF16)
        kall_ref[0, :, sl] = (kv[:, sl] + kr).astype(BF16)
    for pr in range(2):
        _store_v_slots(vall_ref, 2 * pr, kv[:, (4 + pr) * LANES:(5 + pr) * LANES], lo)

    base = MLA_HEADS * LANES
    for blk_i in range(2):
        qb = p[:, C_QB + blk_i * LANES:C_QB + (blk_i + 1) * LANES]
        qb = qb * _head_rms(qb, lane) * pv_ref[2:3, 0:LANES]
        qb = _rope(qb, cb, sb_up, sb_dn, HEAD_DIM // 2) * qscale_h
        qall_ref[0, :, base + (2 * blk_i) * LANES:base + (2 * blk_i + 1) * LANES] = jnp.where(lo, qb, 0.0).astype(BF16)
        qall_ref[0, :, base + (2 * blk_i + 1) * LANES:base + (2 * blk_i + 2) * LANES] = jnp.where(lo, 0.0, qb).astype(BF16)
    kb = p[:, C_KB:C_KB + LANES]
    kb = kb * _head_rms(kb, lane) * pv_ref[3:4, 0:LANES]
    kb = _rope(kb, cb, sb_up, sb_dn, HEAD_DIM // 2)
    k0, k1 = _dup_halves(kb, lane)
    k0 = k0.astype(BF16)
    k1 = k1.astype(BF16)
    kall_ref[0, :, base:base + LANES] = k0
    kall_ref[0, :, base + LANES:base + 2 * LANES] = k0
    kall_ref[0, :, base + 2 * LANES:base + 3 * LANES] = k1
    kall_ref[0, :, base + 3 * LANES:base + 4 * LANES] = k1
    v0, v1 = _dup_halves(p[:, C_VB:C_VB + LANES], lane)
    _store_v_slots(vall_ref, 4, v0, lo)
    _store_v_slots(vall_ref, 6, v1, lo)

    for blk_i in range(2):
        qc = p[:, C_QC + blk_i * LANES:C_QC + (blk_i + 1) * LANES]
        swa_ref[0, :, blk_i * LANES:(blk_i + 1) * LANES] = (
            _rope(qc, cb, sb_up, sb_dn, HEAD_DIM // 2) * qscale_h).astype(BF16)
    kc = _rope(p[:, C_KC:C_KC + LANES], cb, sb_up, sb_dn, HEAD_DIM // 2)
    kc0, kc1 = _dup_halves(kc, lane)
    swa_ref[0, :, 2 * LANES:3 * LANES] = kc0.astype(BF16)
    swa_ref[0, :, 3 * LANES:4 * LANES] = kc1.astype(BF16)
    vc0, vc1 = _dup_halves(p[:, C_VC:C_VC + LANES], lane)
    _store_v_slots(swa_ref, 4, vc0, lo)
    _store_v_slots(swa_ref, 6, vc1, lo)

    g_ref[0, :, 0:256] = _silu(p[:, C_GA:C_GA + 256]).astype(BF16)
    g_ref[0, :, 256:512] = _silu(p[:, C_GB:C_GB + 256]).astype(BF16)
    g_ref[0, :, 512:768] = _silu(p[:, C_GC:C_GC + 256]).astype(BF16)
    g_ref[0, :, 768:1024] = _silu(p[:, C_Z:C_Z + 256]).astype(BF16)
    xbc_ref[0] = p[:, C_XBC:C_XBC + SSD_CONV_DIM]
    dtr = p[:, C_DT:C_DT + LANES] + pv_ref[4:5, 0:LANES]
    dt_ref[0] = jnp.maximum(dtr, 0.0) + jnp.log(1.0 + jnp.exp(-jnp.abs(dtr)))


def _inproj(xall, mods, nw, wcat, wq, wkv, pvec, ropes, n_lat):
    bsz, t, d = xall.shape
    nlt = n_lat // TM
    grid = (bsz, t // TM)
    tok = lambda w: pl.BlockSpec((1, TM, w), lambda b, i: (b, i, 0))
    full = lambda a: pl.BlockSpec(a.shape, lambda b, i: (0,) * a.ndim)
    widths = (8 * LANES, 8 * LANES, 8 * LANES, 8 * LANES, 1024, SSD_CONV_DIM, LANES)
    dtypes = (BF16, BF16, BF16, BF16, BF16, F32, F32)
    return pl.pallas_call(
        _inproj_kernel,
        grid=grid,
        in_specs=[tok(d),
                  pl.BlockSpec((1, 3, d), lambda b, i: (jnp.where(i < nlt, b, bsz), 0, 0)),
                  full(nw), full(wcat), full(wq), full(wkv), full(pvec),
                  pl.BlockSpec((6, TM, LANES), lambda b, i: (0, i, 0))],
        out_specs=[tok(w) for w in widths],
        out_shape=[jax.ShapeDtypeStruct((bsz, t, w), dt) for w, dt in zip(widths, dtypes)],
        compiler_params=_cparams(("parallel", "parallel")),
        name="inproj",
    )(xall, mods, nw, wcat, wq, wkv, pvec, ropes)


def _kv_chunks(nk):
    return [(c0, min(KV_CHUNK, nk - c0)) for c0 in range(0, nk, KV_CHUNK)]


def _logits_pass(q, k_ref, s_ref, sl, nk):
    mrun = None
    for c0, cw in _kv_chunks(nk):
        s = _dot_nt(q, k_ref[0, c0:c0 + cw, sl])
        s_ref[:, c0:c0 + cw] = s
        for t0 in range(0, cw, LANES):
            piece = s[:, t0:t0 + LANES]
            mrun = piece if mrun is None else jnp.maximum(mrun, piece)
    return jnp.max(mrun, axis=-1, keepdims=True)


def _pv_pass(m, v_ref, s_ref, sl, nk):
    acc = None
    for c0, cw in _kv_chunks(nk):
        pr = jnp.exp2(s_ref[:, c0:c0 + cw] - m).astype(BF16)
        o = _dot(pr, v_ref[0, c0:c0 + cw, sl])
        acc = o if acc is None else acc + o
    return acc / pltpu.roll(acc, HEAD_DIM, 1)


def _attn_body(q_ref, k_ref, v_ref, o_ref, s0_ref, s1_ref):
    tq = q_ref.shape[1]
    nk = k_ref.shape[1]
    lane = _lane_iota(tq)
    sls = (slice(0, LANES), slice(LANES, 2 * LANES))
    s_refs = (s0_ref, s1_ref)
    ms = [_logits_pass(q_ref[0, :, sls[j]], k_ref, s_refs[j], sls[j], nk) for j in range(2)]
    outs = [_pv_pass(ms[j], v_ref, s_refs[j], sls[j], nk) for j in range(2)]
    o_ref[0] = jnp.where(lane < HEAD_DIM, outs[0], outs[1]).astype(o_ref.dtype)


def _attn_latent_kernel(q_ref, k_ref, v_ref, o_ref, s0_ref, s1_ref):
    _attn_body(q_ref, k_ref, v_ref, o_ref, s0_ref, s1_ref)


def _attn_ctx_kernel(q_ref, k_ref, v_ref, prev_ref, o_ref, s0_ref, s1_ref):
    del prev_ref
    _attn_body(q_ref, k_ref, v_ref, o_ref, s0_ref, s1_ref)


def _attention_latent(qall, kall, vall, n_lat):
    bsz, t, _ = qall.shape
    return pl.pallas_call(
        _attn_latent_kernel,
        grid=(bsz, 4, n_lat // TQ),
        in_specs=[pl.BlockSpec((1, TQ, 2 * LANES), lambda b, p, i: (b, i, p)),
                  pl.BlockSpec((1, t, 2 * LANES), lambda b, p, i: (b, 0, p)),
                  pl.BlockSpec((1, t, 2 * LANES), lambda b, p, i: (b, 0, p))],
        out_specs=pl.BlockSpec((1, TQ, LANES), lambda b, p, i: (b, i, p)),
        out_shape=jax.ShapeDtypeStruct((bsz, t, 4 * LANES), BF16),
        scratch_shapes=[pltpu.VMEM((TQ, t), F32), pltpu.VMEM((TQ, t), F32)],
        compiler_params=_cparams(("parallel", "parallel", "arbitrary")),
        name="attn_global",
    )(qall, kall, vall)


def _attention_ctx(qall, kall, vall, oab, n_lat, n_ctx):
    bsz, t, _ = qall.shape
    qoff = n_lat // TQ_CTX
    koff = n_lat // n_ctx
    return pl.pallas_call(
        _attn_ctx_kernel,
        grid=(bsz, 4, n_ctx // TQ_CTX),
        in_specs=[pl.BlockSpec((1, TQ_CTX, 2 * LANES), lambda b, p, i: (b, qoff + i, p)),
                  pl.BlockSpec((1, n_ctx, 2 * LANES), lambda b, p, i: (b, koff, p)),
                  pl.BlockSpec((1, n_ctx, 2 * LANES), lambda b, p, i: (b, koff, p)),
                  pl.BlockSpec(memory_space=pl.ANY)],
        out_specs=pl.BlockSpec((1, TQ_CTX, LANES), lambda b, p, i: (b, qoff + i, p)),
        out_shape=jax.ShapeDtypeStruct(oab.shape, oab.dtype),
        scratch_shapes=[pltpu.VMEM((TQ_CTX, n_ctx), F32), pltpu.VMEM((TQ_CTX, n_ctx), F32)],
        input_output_aliases={3: 0},
        compiler_params=_cparams(("parallel", "parallel", "arbitrary")),
        name="attn_global_ctx",
    )(qall, kall, vall, oab)


def _swa_kernel(q_ref, k_ref, v_ref, sink_ref, o_ref, *, n_lat, n_ctx):
    tq = q_ref.shape[1]
    band = tq + 2 * WINDOW
    nkeys = n_ctx + band
    i = pl.program_id(1)
    lane = _lane_iota(tq)
    lo = lane < HEAD_DIM

    start = pl.multiple_of(jnp.clip(i * tq - WINDOW, 0, n_lat - band), WINDOW)
    row = lax.broadcasted_iota(jnp.int32, (tq, nkeys), 0)
    col = lax.broadcasted_iota(jnp.int32, (tq, nkeys), 1)
    dist = (start + col - n_ctx) - (i * tq + row)
    valid = (col < n_ctx) | ((jnp.abs(dist) <= WINDOW) & (i < n_lat // tq))
    k = jnp.concatenate([k_ref[0, n_lat:n_lat + n_ctx, :], k_ref[0, pl.ds(start, band), :]], axis=0)
    v = jnp.concatenate([v_ref[0, n_lat:n_lat + n_ctx, :], v_ref[0, pl.ds(start, band), :]], axis=0)

    logits = []
    for hh in range(SWA_HEADS):
        ps = slice((hh // 2) * LANES, (hh // 2 + 1) * LANES)
        q = q_ref[0, :, ps]
        zero = jnp.zeros_like(q)
        qj = jnp.where(lo, q, zero) if hh % 2 == 0 else jnp.where(lo, zero, q)
        s = jnp.where(valid, _dot_nt(qj, k[:, ps]), NEG_BIG)
        sink = sink_ref[hh:hh + 1, 0:1] * LOG2E
        logits.append((s, sink, jnp.maximum(jnp.max(s, axis=-1, keepdims=True), sink)))
    outs = []
    for hh in range(SWA_HEADS):
        s, sink, m = logits[hh]
        acc = _dot(jnp.exp2(s - m).astype(BF16), v[:, hh * LANES:(hh + 1) * LANES])
        outs.append(acc / (pltpu.roll(acc, HEAD_DIM, 1) + jnp.exp2(sink - m)))
    for pair in range(2):
        o_ref[0, :, pair * LANES:(pair + 1) * LANES] = jnp.where(
            lo, outs[2 * pair], outs[2 * pair + 1]).astype(o_ref.dtype)


def _swa(swa_in, sink_rows, n_lat, n_ctx):
    bsz, t, _ = swa_in.shape
    return pl.pallas_call(
        functools.partial(_swa_kernel, n_lat=n_lat, n_ctx=n_ctx),
        grid=(bsz, t // TQ_SWA),
        in_specs=[pl.BlockSpec((1, TQ_SWA, 2 * LANES), lambda b, i: (b, i, 0)),
                  pl.BlockSpec((1, t, 2 * LANES), lambda b, i: (b, 0, 1)),
                  pl.BlockSpec((1, t, 4 * LANES), lambda b, i: (b, 0, 1)),
                  pl.BlockSpec(sink_rows.shape, lambda b, i: (0, 0))],
        out_specs=pl.BlockSpec((1, TQ_SWA, 2 * LANES), lambda b, i: (b, i, 0)),
        out_shape=jax.ShapeDtypeStruct((bsz, t, 2 * LANES), BF16),
        compiler_params=_cparams(("parallel", "arbitrary")),
        name="attn_window",
    )(swa_in, swa_in, swa_in, sink_rows)


def _conv_kernel(x_ref, prev_ref, next_ref, w_ref, b_ref, u_ref, ext_ref, *, lat_tiles):
    tm = x_ref.shape[1]
    i = pl.program_id(1)
    first = (i == 0) | (i == lat_tiles)
    last = (i == lat_tiles - 1) | (i == pl.num_programs(1) - 1)
    ext_ref[0:HALO, :] = jnp.where(first, 0.0, prev_ref[0])
    ext_ref[HALO:HALO + tm, :] = x_ref[0]
    ext_ref[HALO + tm:2 * HALO + tm, :] = jnp.where(last, 0.0, next_ref[0])
    acc = jnp.zeros((tm, x_ref.shape[2]), F32) + b_ref[...]
    for tap in range(SSD_CONV):
        acc = acc + ext_ref[pl.ds(HALO - SSD_CONV // 2 + tap, tm), :] * w_ref[tap:tap + 1, :]
    u_ref[0] = _silu(acc).astype(u_ref.dtype)


def _conv(xbc, conv_w8, conv_b, n_lat):
    bsz, t, c = xbc.shape
    per = TM // HALO
    nh = t // HALO
    return pl.pallas_call(
        functools.partial(_conv_kernel, lat_tiles=n_lat // TM),
        grid=(bsz, t // TM),
        in_specs=[pl.BlockSpec((1, TM, c), lambda b, i: (b, i, 0)),
                  pl.BlockSpec((1, HALO, c), lambda b, i: (b, jnp.maximum(i * per - 1, 0), 0)),
                  pl.BlockSpec((1, HALO, c), lambda b, i: (b, jnp.minimum((i + 1) * per, nh - 1), 0)),
                  pl.BlockSpec(conv_w8.shape, lambda b, i: (0, 0)),
                  pl.BlockSpec(conv_b.shape, lambda b, i: (0, 0))],
        out_specs=pl.BlockSpec((1, TM, c), lambda b, i: (b, i, 0)),
        out_shape=jax.ShapeDtypeStruct((bsz, t, c), BF16),
        scratch_shapes=[pltpu.VMEM((TM + 2 * HALO, c), F32)],
        compiler_params=_cparams(("parallel", "parallel")),
        name="ssd_conv",
    )(xbc, xbc, xbc, conv_w8, conv_b)


def _ssd_kernel(u_ref, dt_ref, alog_ref, y_ref, st_ref):
    q = u_ref.shape[1]
    direction = pl.program_id(1)
    step = pl.program_id(2)
    fwd = direction == 0

    @pl.when(step == 0)
    def _():
        st_ref[...] = jnp.zeros_like(st_ref)

    lane1 = lax.broadcasted_iota(jnp.int32, (1, LANES), 1)
    avec = jnp.where(lane1 < 2 * SSD_HEADS, -jnp.exp(alog_ref[...]), 0.0)
    dt = dt_ref[0]
    dta = dt * avec

    row = lax.broadcasted_iota(jnp.int32, (q, q), 0)
    col = lax.broadcasted_iota(jnp.int32, (q, q), 1)
    sgn = 1 - 2 * direction
    mask = (row - col) * sgn >= 0
    ones = mask.astype(BF16)
    hi = dta.astype(BF16)
    r1 = dta - hi.astype(F32)
    mid = r1.astype(BF16)
    lo3 = (r1 - mid.astype(F32)).astype(BF16)
    a_cum = _dot(ones, hi) + _dot(ones, mid) + _dot(ones, lo3)
    a_cum_t = a_cum.T
    a_tot = jnp.where(fwd, a_cum[q - 1:q, :], a_cum[0:1, :])

    def col_of(arr, hh):
        return jnp.where(fwd, arr[:, hh:hh + 1], arr[:, SSD_HEADS + hh:SSD_HEADS + hh + 1])

    def row_of(arr, hh):
        return jnp.where(fwd, arr[hh:hh + 1, :], arr[SSD_HEADS + hh:SSD_HEADS + hh + 1, :])

    laneq = _lane_iota(q)
    lo = laneq < SSD_HEAD_DIM
    lo1 = lane1 < SSD_HEAD_DIM
    for g in range(SSD_GROUPS):
        h0, h1 = 2 * g, 2 * g + 1
        xs = u_ref[0, :, g * LANES:(g + 1) * LANES].astype(F32)
        bg = u_ref[0, :, SSD_INNER + g * SSD_STATE:SSD_INNER + (g + 1) * SSD_STATE]
        cg = u_ref[0, :, SSD_INNER + (SSD_GROUPS + g) * SSD_STATE:SSD_INNER + (SSD_GROUPS + g + 1) * SSD_STATE]
        gmat = _dot_nt(cg, bg)
        acl = jnp.where(lo, col_of(a_cum, h0), col_of(a_cum, h1))
        dtl = jnp.where(lo, col_of(dt, h0), col_of(dt, h1))
        atl = jnp.where(lo1, col_of(a_tot, h0), col_of(a_tot, h1))
        xdt = xs * dtl
        xdt_b = xdt.astype(BF16)
        ys = []
        for hh in (h0, h1):
            seg = col_of(a_cum, hh) - row_of(a_cum_t, hh)
            dec = jnp.exp(jnp.where(mask, seg, NEG_BIG))
            ys.append(_dot((gmat * dec).astype(BF16), xdt_b))
        st = st_ref[g]
        y = jnp.where(lo, ys[0], ys[1]) + _dot(cg, st.astype(BF16)) * jnp.exp(acl)
        y_ref[0, 0, :, g * LANES:(g + 1) * LANES] = y
        w = (xdt * jnp.exp(atl - acl)).astype(BF16)
        st_ref[g] = st * jnp.exp(atl) + _dot_tn(bg, w)


def _ssd(u, dt, alog_row, n_lat):
    bsz, t, c = u.shape
    nc = t // SSD_Q
    ncl = n_lat // SSD_Q
    ncc = nc - ncl

    def chunk(d, s):
        fwd = jnp.where(s < ncc, ncl + s, s - ncc)
        bwd = jnp.where(s < ncc, nc - 1 - s, ncl - 1 - (s - ncc))
        return jnp.where(d == 0, fwd, bwd)

    return pl.pallas_call(
        _ssd_kernel,
        grid=(bsz, 2, nc),
        in_specs=[pl.BlockSpec((1, SSD_Q, c), lambda b, d, s: (b, chunk(d, s), 0)),
                  pl.BlockSpec((1, SSD_Q, LANES), lambda b, d, s: (b, chunk(d, s), 0)),
                  pl.BlockSpec((1, LANES), lambda b, d, s: (0, 0))],
        out_specs=pl.BlockSpec((1, 1, SSD_Q, SSD_INNER), lambda b, d, s: (b, d, chunk(d, s), 0)),
        out_shape=jax.ShapeDtypeStruct((bsz, 2, t, SSD_INNER), F32),
        scratch_shapes=[pltpu.VMEM((SSD_GROUPS, SSD_STATE, LANES), F32)],
        compiler_params=_cparams(("parallel", "parallel", "arbitrary")),
        name="ssd_scan",
    )(u, dt, alog_row)


def _outproj_kernel(oab_ref, oc_ref, y_ref, u_ref, g_ref, x_ref, mod_ref, pv_ref, w_ref, fw_ref, o_ref, *, final):
    g = g_ref[0].astype(F32)
    oab = oab_ref[0].astype(F32)
    m_a = oab[:, 0:256] * g[:, 0:256]
    m_b = oab[:, 256:512] * g[:, 256:512]
    m_c = oc_ref[0].astype(F32) * g[:, 512:768]
    yd = pv_ref[0:1, :] * u_ref[0].astype(F32) + y_ref[0, 0] + y_ref[0, 1]
    td = yd * g[:, 768:1024]
    m_d = td * _rms(td, SSD_INNER) * pv_ref[1:2, :]
    acc = _dot(m_a.astype(BF16), w_ref[0:256, :])
    acc += _dot(m_b.astype(BF16), w_ref[256:512, :])
    acc += _dot(m_c.astype(BF16), w_ref[512:768, :])
    acc += _dot(m_d.astype(BF16), w_ref[768:1024, :])
    xn = x_ref[0] + mod_ref[0, 2:3, :] * acc
    if final:
        xn = xn * _rms(xn, xn.shape[-1]) * fw_ref[...]
    o_ref[0] = xn


def _outproj(oab, oc, y, u, gates, xall, mods, pvec, wout, fw, n_lat, final):
    bsz, t, d = xall.shape
    nlt = n_lat // TM
    rows = n_lat if final else t
    tok = lambda w: pl.BlockSpec((1, TM, w), lambda b, i: (b, i, 0))
    full = lambda a: pl.BlockSpec(a.shape, lambda b, i: (0,) * a.ndim)
    return pl.pallas_call(
        functools.partial(_outproj_kernel, final=final),
        grid=(bsz, rows // TM),
        in_specs=[tok(4 * LANES), tok(2 * LANES),
                  pl.BlockSpec((1, 2, TM, SSD_INNER), lambda b, i: (b, 0, i, 0)),
                  tok(SSD_INNER), tok(1024), tok(d),
                  pl.BlockSpec((1, 3, d), lambda b, i: (jnp.where(i < nlt, b, bsz), 0, 0)),
                  full(pvec), full(wout), full(fw)],
        out_specs=tok(d),
        out_shape=jax.ShapeDtypeStruct((bsz, rows, d), F32),
        compiler_params=_cparams(("parallel", "parallel")),
        name="outproj",
    )(oab, oc, y, u, gates, xall, mods, pvec, wout, fw)


def _rope_tables(n_ctx, n_lat):
    rows = n_lat // GRID_W
    row = jnp.repeat(jnp.arange(rows, dtype=F32), GRID_W)
    col = jnp.tile(jnp.arange(GRID_W, dtype=F32), rows)

    def cos_sin(rot_dim):
        n_freq = rot_dim // 4
        inv_freq = ROPE_THETA ** (-jnp.arange(n_freq, dtype=F32) / n_freq)
        ang = jnp.concatenate([row[:, None] * inv_freq, col[:, None] * inv_freq], axis=-1)
        return jnp.cos(ang), jnp.sin(ang)

    def with_ctx(tab, fill):
        return jnp.concatenate([tab, jnp.full((n_ctx, LANES), fill, F32)], axis=0)

    lane = np.arange(LANES)
    cos_a, sin_a = cos_sin(MLA_ROPE)
    half = MLA_ROPE // 2
    idx = np.where((lane >= MLA_NOPE) & (lane < MLA_NOPE + MLA_ROPE), (lane - MLA_NOPE) % half, 0)
    first = (lane >= MLA_NOPE) & (lane < MLA_NOPE + half)
    second = (lane >= MLA_NOPE + half) & (lane < MLA_NOPE + MLA_ROPE)
    rot = first | second
    ca = jnp.where(rot, cos_a[:, idx], 1.0)
    sa_up = jnp.where(first, -sin_a[:, idx], 0.0)
    sa_dn = jnp.where(second, sin_a[:, idx], 0.0)
    cos_h, sin_h = cos_sin(HEAD_DIM)
    half = HEAD_DIM // 2
    idx = lane % half
    first = (lane % HEAD_DIM) < half
    cb = cos_h[:, idx]
    sb_up = jnp.where(first, -sin_h[:, idx], 0.0)
    sb_dn = jnp.where(first, 0.0, sin_h[:, idx])
    return jnp.stack([with_ctx(ca, 1.0), with_ctx(sa_up, 0.0), with_ctx(sa_dn, 0.0),
                      with_ctx(cb, 1.0), with_ctx(sb_up, 0.0), with_ctx(sb_dn, 0.0)])


def _relayout_w_in(w_in):
    depth, d, _ = w_in.shape
    widths = (MLA_Q_RANK, MLA_KV_RANK, MLA_ROPE, MLA_HEADS * MLA_V,
              256, 128, 128, 256, 256, 128, 128, 256,
              SSD_INNER, SSD_CONV_DIM, 2 * SSD_HEADS)
    parts = jnp.split(w_in, [int(v) for v in np.cumsum(widths)[:-1]], axis=-1)
    z = lambda n: jnp.zeros((depth, d, n), w_in.dtype)
    kr = jnp.concatenate([z(MLA_NOPE), parts[2], z(LANES - MLA_NOPE - MLA_ROPE)], axis=-1)
    dtp = jnp.concatenate([parts[14], z(LANES - 2 * SSD_HEADS)], axis=-1)
    cat = jnp.concatenate(parts[0:2] + [kr] + parts[3:14] + [dtp], axis=-1)
    assert cat.shape[-1] == C_END
    return cat.astype(BF16)


def _relayout_mla(w_q_up, w_kv_up):
    depth = w_q_up.shape[0]
    wq = w_q_up.reshape(depth, MLA_Q_RANK, MLA_HEADS, MLA_NOPE + MLA_ROPE)
    wq = jnp.pad(wq, ((0, 0), (0, 0), (0, 0), (0, LANES - MLA_NOPE - MLA_ROPE)))
    wq = wq.reshape(depth, MLA_Q_RANK, MLA_HEADS * LANES)
    wkv = w_kv_up.reshape(depth, MLA_KV_RANK, MLA_HEADS, MLA_NOPE + MLA_V)
    wk = jnp.pad(wkv[..., :MLA_NOPE], ((0, 0), (0, 0), (0, 0), (0, LANES - MLA_NOPE)))
    wk = wk.reshape(depth, MLA_KV_RANK, MLA_HEADS * LANES)
    wv = wkv[..., MLA_NOPE:].reshape(depth, MLA_KV_RANK, MLA_HEADS * MLA_V)
    return wq.astype(BF16), jnp.concatenate([wk, wv], axis=-1).astype(BF16)


def _pad_lanes(v, width):
    return jnp.pad(v, ((0, 0), (0, width - v.shape[-1])))


def kernel(x, c, ctx, c_ctx, norm_w, w_mod, b_mod, w_in, mla_q_norm, mla_kv_norm, mla_w_q_up, mla_w_kv_up,
           gqa_q_norm, gqa_k_norm, swa_sink, ssd_conv_w, ssd_conv_b, ssd_a_log, ssd_dt_bias, ssd_d,
           ssd_norm_w, w_out, final_norm_w):
    bsz, n_lat, d = x.shape
    n_ctx = ctx.shape[1]
    depth = w_in.shape[0]
    assert d == 1024 and bsz + 1 <= 8
    assert n_ctx % TM == 0 and n_ctx % TQ_CTX == 0 and n_ctx % TQ_SWA == 0 and n_ctx % SSD_Q == 0
    assert n_lat % TQ == 0 and n_lat % n_ctx == 0 and n_lat % GRID_W == 0
    assert n_lat >= TQ_SWA + 2 * WINDOW

    xall = jnp.concatenate([x, ctx], axis=1)
    cvec = jnp.concatenate([c, c_ctx[None, :], jnp.zeros((8 - bsz - 1, d), F32)], axis=0)
    mods = _modulation(cvec, w_mod, b_mod).reshape(depth, 8, 3, d)[:, :bsz + 1]

    wcat = _relayout_w_in(w_in)
    wq, wkv = _relayout_mla(mla_w_q_up, mla_w_kv_up)
    wout = w_out.astype(BF16)
    ropes = _rope_tables(n_ctx, n_lat)
    pv_in = jnp.stack([mla_q_norm,
                       _pad_lanes(mla_kv_norm, 256),
                       jnp.tile(gqa_q_norm, (1, 4)),
                       _pad_lanes(jnp.tile(gqa_k_norm, (1, 2)), 256),
                       _pad_lanes(ssd_dt_bias.reshape(depth, -1), 256)], axis=1)
    pv_in = jnp.pad(pv_in, ((0, 0), (0, 3), (0, 0)))
    pv_out = jnp.stack([jnp.repeat(ssd_d, SSD_HEAD_DIM, axis=-1), ssd_norm_w], axis=1)
    pv_out = jnp.pad(pv_out, ((0, 0), (0, 6), (0, 0)))
    sink_rows = jnp.broadcast_to(jnp.pad(swa_sink, ((0, 0), (0, 4)))[:, :, None], (depth, 8, LANES))
    conv_w8 = jnp.pad(ssd_conv_w, ((0, 0), (0, 8 - SSD_CONV), (0, 0)))
    alog = _pad_lanes(ssd_a_log.reshape(depth, -1), LANES)
    fw = final_norm_w.reshape(1, d)

    for l in range(depth):
        final = l == depth - 1
        qall, kall, vall, swa_in, gates, xbc, dt = _inproj(
            xall, mods[l], norm_w[l].reshape(1, d), wcat[l], wq[l], wkv[l], pv_in[l], ropes, n_lat)
        oab = _attention_latent(qall, kall, vall, n_lat)
        if not final:
            oab = _attention_ctx(qall, kall, vall, oab, n_lat, n_ctx)
        oc = _swa(swa_in, sink_rows[l], n_lat, n_ctx)
        u = _conv(xbc, conv_w8[l], ssd_conv_b[l].reshape(1, -1), n_lat)
        y = _ssd(u, dt, alog[l].reshape(1, LANES), n_lat)
        xall = _outproj(oab, oc, y, u, gates, xall, mods[l], pv_out[l], wout[l], fw, n_lat, final)
    return xall
```

```python
import functools

import numpy as np
import jax
import jax.numpy as jnp
from jax import lax
from jax.experimental import pallas as pl
from jax.experimental.pallas import tpu as pltpu

F32 = jnp.float32
BF16 = jnp.bfloat16

EPS = 1e-6
GRID_W = 64
ROPE_THETA = 10000.0
HEAD_DIM = 64
MLA_HEADS, MLA_NOPE, MLA_ROPE, MLA_V = 4, 64, 32, 64
MLA_Q_RANK, MLA_KV_RANK = 256, 128
GQA_HEADS, GQA_KV_HEADS = 4, 2
SWA_HEADS, SWA_KV_HEADS = 4, 2
WINDOW = 128
SSD_HEADS, SSD_HEAD_DIM, SSD_GROUPS, SSD_STATE, SSD_CONV = 4, 64, 2, 128, 5
SSD_INNER = SSD_HEADS * SSD_HEAD_DIM
SSD_CONV_DIM = SSD_INNER + 2 * SSD_GROUPS * SSD_STATE

LANES = 128
LOG2E = 1.4426950408889634
NEG_BIG = -1e30

C_CQ, C_CKV, C_KR, C_GA = 0, 256, 384, 512
C_QB, C_KB, C_VB, C_GB = 768, 1024, 1152, 1280
C_QC, C_KC, C_VC, C_GC = 1536, 1792, 1920, 2048
C_Z, C_XBC, C_DT, C_END = 2304, 2560, 3328, 3456

TM = 256
NB_IN = 2
TQ = 512
TQ_CTX = 256
KV_CHUNK = 1024
TQ_SWA = 256
SSD_Q = 256
HALO = 8
VMEM_LIMIT = 56 * 1024 * 1024


def _cparams(sem):
    return pltpu.CompilerParams(dimension_semantics=sem, vmem_limit_bytes=VMEM_LIMIT)


def _sigmoid(x):
    return 1.0 / (1.0 + jnp.exp(-x))


def _silu(x):
    return x * _sigmoid(x)


def _rms(x, n):
    return lax.rsqrt(jnp.sum(x * x, axis=-1, keepdims=True) * (1.0 / n) + EPS)


def _dot(a, b):
    return jnp.dot(a, b, preferred_element_type=F32)


def _dot_nt(a, b):
    return lax.dot_general(a, b, (((1,), (1,)), ((), ())), preferred_element_type=F32)


def _dot_tn(a, b):
    return lax.dot_general(a, b, (((0,), (0,)), ((), ())), preferred_element_type=F32)


def _lane_iota(rows):
    return lax.broadcasted_iota(jnp.int32, (rows, LANES), 1)


def _mod_kernel(c_ref, w_ref, b_ref, o_ref):
    cs = _silu(c_ref[...])
    o_ref[0] = _dot(cs.astype(BF16), w_ref[0].astype(BF16)) + b_ref[0]


def _modulation(cvec, w_mod, b_mod):
    depth, d, d3 = w_mod.shape
    tn = 512
    return pl.pallas_call(
        _mod_kernel,
        grid=(depth, d3 // tn),
        in_specs=[pl.BlockSpec((8, d), lambda l, j: (0, 0)),
                  pl.BlockSpec((1, d, tn), lambda l, j: (l, 0, j)),
                  pl.BlockSpec((1, 1, tn), lambda l, j: (l, 0, j))],
        out_specs=pl.BlockSpec((1, 8, tn), lambda l, j: (l, 0, j)),
        out_shape=jax.ShapeDtypeStruct((depth, 8, d3), F32),
        compiler_params=_cparams(("parallel", "parallel")),
        name="modulation",
    )(cvec, w_mod, b_mod.reshape(depth, 1, d3))


def _rope(x, c, s_up, s_dn, half):
    return x * c + pltpu.roll(x, LANES - half, 1) * s_up + pltpu.roll(x, half, 1) * s_dn


def _head_rms(blk, lane):
    sq = blk * blk
    lo = lane < HEAD_DIM
    s_lo = jnp.sum(jnp.where(lo, sq, 0.0), axis=-1, keepdims=True)
    s_hi = jnp.sum(jnp.where(lo, 0.0, sq), axis=-1, keepdims=True)
    r_lo = lax.rsqrt(s_lo * (1.0 / HEAD_DIM) + EPS)
    r_hi = lax.rsqrt(s_hi * (1.0 / HEAD_DIM) + EPS)
    return jnp.where(lo, r_lo, r_hi)


def _dup_halves(blk, lane):
    sw = pltpu.roll(blk, HEAD_DIM, 1)
    lo = lane < HEAD_DIM
    return jnp.where(lo, blk, sw), jnp.where(lo, sw, blk)


def _store_v_slots(v_ref, bb, slot, pair, lo):
    v_ref[bb, :, slot * LANES:(slot + 1) * LANES] = jnp.where(lo, pair, 1.0).astype(BF16)
    v_ref[bb, :, (slot + 1) * LANES:(slot + 2) * LANES] = jnp.where(lo, 1.0, pair).astype(BF16)


def _inproj_kernel(x_ref, mod_ref, nw_ref, wcat_ref, wq_ref, wkv_ref, pv_ref, rope_ref,
                   qall_ref, kall_ref, vall_ref, swa_ref, g_ref, xbc_ref, dt_ref):
    nb, tm, d = x_ref.shape
    lane = _lane_iota(tm)
    lo = lane < HEAD_DIM

    hs = []
    for bb in range(nb):
        x = x_ref[bb]
        xn = x * _rms(x, d)
        h = (xn * nw_ref[...]) * (1.0 + mod_ref[bb, 1:2, :]) + mod_ref[bb, 0:1, :]
        hs.append(h.astype(BF16))
    hcat = jnp.concatenate(hs, axis=0)

    bounds = (C_CQ, C_QB, C_QC, C_Z, C_END)
    proj = {}

    def project(gi):
        proj[gi] = _dot(hcat, wcat_ref[:, bounds[gi]:bounds[gi + 1]])

    def cols(bb, c0, width):
        gi = max(i for i in range(4) if bounds[i] <= c0)
        off = c0 - bounds[gi]
        rows = slice(None) if bb is None else slice(bb * tm, (bb + 1) * tm)
        return proj[gi][rows, off:off + width]

    ca, sa_up, sa_dn = rope_ref[0], rope_ref[1], rope_ref[2]
    cb, sb_up, sb_dn = rope_ref[3], rope_ref[4], rope_ref[5]
    qscale_a = (MLA_NOPE + MLA_ROPE) ** -0.5 * LOG2E
    qscale_h = HEAD_DIM ** -0.5 * LOG2E
    base = MLA_HEADS * LANES

    def stage_a():
        cq = cols(None, C_CQ, MLA_Q_RANK)
        cqn = cq * _rms(cq, MLA_Q_RANK) * pv_ref[0:1, :]
        qa_all = _dot(cqn.astype(BF16), wq_ref[...])
        ckv = cols(None, C_CKV, MLA_KV_RANK)
        ckvn = ckv * _rms(ckv, MLA_KV_RANK) * pv_ref[1:2, 0:MLA_KV_RANK]
        kv_all = _dot(ckvn.astype(BF16), wkv_ref[...])
        for bb in range(nb):
            rows = slice(bb * tm, (bb + 1) * tm)
            qa, kv = qa_all[rows], kv_all[rows]
            kr = _rope(cols(bb, C_KR, LANES), ca, sa_up, sa_dn, MLA_ROPE // 2)
            for hh in range(MLA_HEADS):
                sl = slice(hh * LANES, (hh + 1) * LANES)
                qall_ref[bb, :, sl] = (_rope(qa[:, sl], ca, sa_up, sa_dn, MLA_ROPE // 2) * qscale_a).astype(BF16)
                kall_ref[bb, :, sl] = (kv[:, sl] + kr).astype(BF16)
            for pr in range(2):
                _store_v_slots(vall_ref, bb, 2 * pr, kv[:, (4 + pr) * LANES:(5 + pr) * LANES], lo)
            g_ref[bb, :, 0:256] = _silu(cols(bb, C_GA, 256)).astype(BF16)

    def stage_b(bb):
        for blk_i in range(2):
            qb = cols(bb, C_QB + blk_i * LANES, LANES)
            qb = qb * _head_rms(qb, lane) * pv_ref[2:3, 0:LANES]
            qb = _rope(qb, cb, sb_up, sb_dn, HEAD_DIM // 2) * qscale_h
            c0 = base + 2 * blk_i * LANES
            qall_ref[bb, :, c0:c0 + LANES] = jnp.where(lo, qb, 0.0).astype(BF16)
            qall_ref[bb, :, c0 + LANES:c0 + 2 * LANES] = jnp.where(lo, 0.0, qb).astype(BF16)
        kb = cols(bb, C_KB, LANES)
        kb = kb * _head_rms(kb, lane) * pv_ref[3:4, 0:LANES]
        kb = _rope(kb, cb, sb_up, sb_dn, HEAD_DIM // 2)
        k0, k1 = _dup_halves(kb, lane)
        k0 = k0.astype(BF16)
        k1 = k1.astype(BF16)
        kall_ref[bb, :, base:base + LANES] = k0
        kall_ref[bb, :, base + LANES:base + 2 * LANES] = k0
        kall_ref[bb, :, base + 2 * LANES:base + 3 * LANES] = k1
        kall_ref[bb, :, base + 3 * LANES:base + 4 * LANES] = k1
        v0, v1 = _dup_halves(cols(bb, C_VB, LANES), lane)
        _store_v_slots(vall_ref, bb, 4, v0, lo)
        _store_v_slots(vall_ref, bb, 6, v1, lo)
        g_ref[bb, :, 256:512] = _silu(cols(bb, C_GB, 256)).astype(BF16)

    def stage_c(bb):
        for blk_i in range(2):
            qc = cols(bb, C_QC + blk_i * LANES, LANES)
            swa_ref[bb, :, blk_i * LANES:(blk_i + 1) * LANES] = (
                _rope(qc, cb, sb_up, sb_dn, HEAD_DIM // 2) * qscale_h).astype(BF16)
        kc = _rope(cols(bb, C_KC, LANES), cb, sb_up, sb_dn, HEAD_DIM // 2)
        kc0, kc1 = _dup_halves(kc, lane)
        swa_ref[bb, :, 2 * LANES:3 * LANES] = kc0.astype(BF16)
        swa_ref[bb, :, 3 * LANES:4 * LANES] = kc1.astype(BF16)
        vc0, vc1 = _dup_halves(cols(bb, C_VC, LANES), lane)
        _store_v_slots(swa_ref, bb, 4, vc0, lo)
        _store_v_slots(swa_ref, bb, 6, vc1, lo)
        g_ref[bb, :, 512:768] = _silu(cols(bb, C_GC, 256)).astype(BF16)

    def stage_d(bb):
        g_ref[bb, :, 768:1024] = _silu(cols(bb, C_Z, 256)).astype(BF16)
        xbc_ref[bb] = cols(bb, C_XBC, SSD_CONV_DIM)
        dtr = cols(bb, C_DT, LANES) + pv_ref[4:5, 0:LANES]
        dt_ref[bb] = jnp.maximum(dtr, 0.0) + jnp.log(1.0 + jnp.exp(-jnp.abs(dtr)))

    project(0)
    project(1)
    stage_a()
    project(2)
    for bb in range(nb):
        stage_b(bb)
    project(3)
    for bb in range(nb):
        stage_c(bb)
    for bb in range(nb):
        stage_d(bb)


def _inproj(xall, mods, nw, wcat, wq, wkv, pvec, ropes, n_lat):
    bsz, t, d = xall.shape
    nlt = n_lat // TM
    grid = (bsz // NB_IN, t // TM)
    tok = lambda w: pl.BlockSpec((NB_IN, TM, w), lambda b, i: (b, i, 0))
    full = lambda a: pl.BlockSpec(a.shape, lambda b, i: (0,) * a.ndim)
    widths = (8 * LANES, 8 * LANES, 8 * LANES, 8 * LANES, 1024, SSD_CONV_DIM, LANES)
    dtypes = (BF16, BF16, BF16, BF16, BF16, F32, F32)
    return pl.pallas_call(
        _inproj_kernel,
        grid=grid,
        in_specs=[tok(d),
                  pl.BlockSpec((NB_IN, 3, d), lambda b, i: (jnp.where(i < nlt, b, bsz // NB_IN), 0, 0)),
                  full(nw), full(wcat), full(wq), full(wkv), full(pvec),
                  pl.BlockSpec((6, TM, LANES), lambda b, i: (0, i, 0))],
        out_specs=[tok(w) for w in widths],
        out_shape=[jax.ShapeDtypeStruct((bsz, t, w), dt) for w, dt in zip(widths, dtypes)],
        compiler_params=_cparams(("parallel", "parallel")),
        name="inproj",
    )(xall, mods, nw, wcat, wq, wkv, pvec, ropes)


def _kv_chunks(nk):
    return [(c0, min(KV_CHUNK, nk - c0)) for c0 in range(0, nk, KV_CHUNK)]


def _logits_pass(q, k_ref, s_ref, sl, nk):
    mrun = None
    for c0, cw in _kv_chunks(nk):
        s = _dot_nt(q, k_ref[0, c0:c0 + cw, sl])
        s_ref[:, c0:c0 + cw] = s
        for t0 in range(0, cw, LANES):
            piece = s[:, t0:t0 + LANES]
            mrun = piece if mrun is None else jnp.maximum(mrun, piece)
    return jnp.max(mrun, axis=-1, keepdims=True)


def _pv_pass(m, v_ref, s_ref, sl, nk):
    acc = None
    for c0, cw in _kv_chunks(nk):
        pr = jnp.exp2(s_ref[:, c0:c0 + cw] - m).astype(BF16)
        o = _dot(pr, v_ref[0, c0:c0 + cw, sl])
        acc = o if acc is None else acc + o
    return acc / pltpu.roll(acc, HEAD_DIM, 1)


def _attn_body(q_ref, k_ref, v_ref, o_ref, s0_ref, s1_ref):
    tq = q_ref.shape[1]
    nk = k_ref.shape[1]
    lane = _lane_iota(tq)
    sls = (slice(0, LANES), slice(LANES, 2 * LANES))
    s_refs = (s0_ref, s1_ref)
    ms = [_logits_pass(q_ref[0, :, sls[j]], k_ref, s_refs[j], sls[j], nk) for j in range(2)]
    outs = [_pv_pass(ms[j], v_ref, s_refs[j], sls[j], nk) for j in range(2)]
    o_ref[0] = jnp.where(lane < HEAD_DIM, outs[0], outs[1]).astype(o_ref.dtype)


def _attn_latent_kernel(q_ref, k_ref, v_ref, o_ref, s0_ref, s1_ref):
    _attn_body(q_ref, k_ref, v_ref, o_ref, s0_ref, s1_ref)


def _attn_ctx_kernel(q_ref, k_ref, v_ref, prev_ref, o_ref, s0_ref, s1_ref):
    del prev_ref
    _attn_body(q_ref, k_ref, v_ref, o_ref, s0_ref, s1_ref)


def _attention_latent(qall, kall, vall, n_lat):
    bsz, t, _ = qall.shape
    return pl.pallas_call(
        _attn_latent_kernel,
        grid=(bsz, 4, n_lat // TQ),
        in_specs=[pl.BlockSpec((1, TQ, 2 * LANES), lambda b, p, i: (b, i, p)),
                  pl.BlockSpec((1, t, 2 * LANES), lambda b, p, i: (b, 0, p)),
                  pl.BlockSpec((1, t, 2 * LANES), lambda b, p, i: (b, 0, p))],
        out_specs=pl.BlockSpec((1, TQ, LANES), lambda b, p, i: (b, i, p)),
        out_shape=jax.ShapeDtypeStruct((bsz, t, 4 * LANES), BF16),
        scratch_shapes=[pltpu.VMEM((TQ, t), F32), pltpu.VMEM((TQ, t), F32)],
        compiler_params=_cparams(("parallel", "parallel", "arbitrary")),
        name="attn_global",
    )(qall, kall, vall)


def _attention_ctx(qall, kall, vall, oab, n_lat, n_ctx):
    bsz, t, _ = qall.shape
    qoff = n_lat // TQ_CTX
    koff = n_lat // n_ctx
    return pl.pallas_call(
        _attn_ctx_kernel,
        grid=(bsz, 4, n_ctx // TQ_CTX),
        in_specs=[pl.BlockSpec((1, TQ_CTX, 2 * LANES), lambda b, p, i: (b, qoff + i, p)),
                  pl.BlockSpec((1, n_ctx, 2 * LANES), lambda b, p, i: (b, koff, p)),
                  pl.BlockSpec((1, n_ctx, 2 * LANES), lambda b, p, i: (b, koff, p)),
                  pl.BlockSpec(memory_space=pl.ANY)],
        out_specs=pl.BlockSpec((1, TQ_CTX, LANES), lambda b, p, i: (b, qoff + i, p)),
        out_shape=jax.ShapeDtypeStruct(oab.shape, oab.dtype),
        scratch_shapes=[pltpu.VMEM((TQ_CTX, n_ctx), F32), pltpu.VMEM((TQ_CTX, n_ctx), F32)],
        input_output_aliases={3: 0},
        compiler_params=_cparams(("parallel", "parallel", "arbitrary")),
        name="attn_global_ctx",
    )(qall, kall, vall, oab)


def _swa_kernel(q_ref, k_ref, v_ref, bias_ref, sink_ref, o_ref, *, n_lat, n_ctx):
    tq = q_ref.shape[1]
    band = tq + 2 * WINDOW
    i = pl.program_id(1)
    lane = _lane_iota(tq)
    lo = lane < HEAD_DIM

    start = pl.multiple_of(jnp.clip(i * tq - WINDOW, 0, n_lat - band), WINDOW)
    bias = bias_ref[0]
    k = jnp.concatenate([k_ref[0, n_lat:n_lat + n_ctx, :], k_ref[0, pl.ds(start, band), :]], axis=0)
    v = jnp.concatenate([v_ref[0, n_lat:n_lat + n_ctx, :], v_ref[0, pl.ds(start, band), :]], axis=0)

    def logits_of(hh):
        ps = slice((hh // 2) * LANES, (hh // 2 + 1) * LANES)
        q = q_ref[0, :, ps]
        zero = jnp.zeros_like(q)
        qj = jnp.where(lo, q, zero) if hh % 2 == 0 else jnp.where(lo, zero, q)
        s = _dot_nt(qj, k[:, ps]) + bias
        sink = sink_ref[hh:hh + 1, 0:1] * LOG2E
        return s, sink, jnp.maximum(jnp.max(s, axis=-1, keepdims=True), sink)

    ahead = 2
    logits = [logits_of(hh) for hh in range(ahead)]
    outs = []
    for hh in range(SWA_HEADS):
        if hh + ahead < SWA_HEADS:
            logits.append(logits_of(hh + ahead))
        s, sink, m = logits[hh]
        acc = _dot(jnp.exp2(s - m).astype(BF16), v[:, hh * LANES:(hh + 1) * LANES])
        outs.append(acc / (pltpu.roll(acc, HEAD_DIM, 1) + jnp.exp2(sink - m)))
    for pair in range(2):
        o_ref[0, :, pair * LANES:(pair + 1) * LANES] = jnp.where(
            lo, outs[2 * pair], outs[2 * pair + 1]).astype(o_ref.dtype)


def _swa_bias(n_ctx):
    band = TQ_SWA + 2 * WINDOW
    r = np.arange(TQ_SWA)[:, None]
    c = np.arange(band)[None, :]
    ok = [np.abs(c - shift - r) <= WINDOW for shift in (0, WINDOW, 2 * WINDOW)] + [np.zeros((TQ_SWA, band), bool)]
    lat = np.where(np.stack(ok), 0.0, NEG_BIG).astype(np.float32)
    return jnp.asarray(np.concatenate([np.zeros((4, TQ_SWA, n_ctx), np.float32), lat], axis=-1))


def _swa(swa_in, sink_rows, n_lat, n_ctx):
    bsz, t, _ = swa_in.shape
    nlt = n_lat // TQ_SWA
    bias = _swa_bias(n_ctx)

    def kind(i):
        return jnp.where(i >= nlt, 3, jnp.where(i == 0, 0, jnp.where(i == nlt - 1, 2, 1)))

    return pl.pallas_call(
        functools.partial(_swa_kernel, n_lat=n_lat, n_ctx=n_ctx),
        grid=(bsz, t // TQ_SWA),
        in_specs=[pl.BlockSpec((1, TQ_SWA, 2 * LANES), lambda b, i: (b, i, 0)),
                  pl.BlockSpec((1, t, 2 * LANES), lambda b, i: (b, 0, 1)),
                  pl.BlockSpec((1, t, 4 * LANES), lambda b, i: (b, 0, 1)),
                  pl.BlockSpec((1,) + bias.shape[1:], lambda b, i: (kind(i), 0, 0)),
                  pl.BlockSpec(sink_rows.shape, lambda b, i: (0, 0))],
        out_specs=pl.BlockSpec((1, TQ_SWA, 2 * LANES), lambda b, i: (b, i, 0)),
        out_shape=jax.ShapeDtypeStruct((bsz, t, 2 * LANES), BF16),
        compiler_params=_cparams(("parallel", "arbitrary")),
        name="attn_window",
    )(swa_in, swa_in, swa_in, bias, sink_rows)


def _conv_kernel(x_ref, prev_ref, next_ref, w_ref, b_ref, u_ref, *, lat_tiles):
    tm = x_ref.shape[1]
    i = pl.program_id(1)
    first = (i == 0) | (i == lat_tiles)
    last = (i == lat_tiles - 1) | (i == pl.num_programs(1) - 1)
    ext = jnp.concatenate([jnp.where(first, 0.0, prev_ref[0]), x_ref[0], jnp.where(last, 0.0, next_ref[0])], axis=0)
    n_ext = tm + 2 * HALO
    acc = x_ref[0] * w_ref[SSD_CONV // 2:SSD_CONV // 2 + 1, :] + b_ref[...]
    for tap in range(SSD_CONV):
        if tap != SSD_CONV // 2:
            shifted = pltpu.roll(ext, (SSD_CONV // 2 - tap) % n_ext, 0)
            acc = acc + shifted[HALO:HALO + tm, :] * w_ref[tap:tap + 1, :]
    u_ref[0] = _silu(acc).astype(u_ref.dtype)


def _conv(xbc, conv_w8, conv_b, n_lat):
    bsz, t, c = xbc.shape
    per = TM // HALO
    nh = t // HALO
    return pl.pallas_call(
        functools.partial(_conv_kernel, lat_tiles=n_lat // TM),
        grid=(bsz, t // TM),
        in_specs=[pl.BlockSpec((1, TM, c), lambda b, i: (b, i, 0)),
                  pl.BlockSpec((1, HALO, c), lambda b, i: (b, jnp.maximum(i * per - 1, 0), 0)),
                  pl.BlockSpec((1, HALO, c), lambda b, i: (b, jnp.minimum((i + 1) * per, nh - 1), 0)),
                  pl.BlockSpec(conv_w8.shape, lambda b, i: (0, 0)),
                  pl.BlockSpec(conv_b.shape, lambda b, i: (0, 0))],
        out_specs=pl.BlockSpec((1, TM, c), lambda b, i: (b, i, 0)),
        out_shape=jax.ShapeDtypeStruct((bsz, t, c), BF16),
        compiler_params=_cparams(("parallel", "parallel")),
        name="ssd_conv",
    )(xbc, xbc, xbc, conv_w8, conv_b)


def _ssd_kernel(uf_ref, ub_ref, dtf_ref, dtb_ref, alog_ref, yf_ref, yb_ref, st_ref):
    q = uf_ref.shape[1]
    u_refs, dt_refs, y_refs = (uf_ref, ub_ref), (dtf_ref, dtb_ref), (yf_ref, yb_ref)

    @pl.when(pl.program_id(1) == 0)
    def _():
        st_ref[...] = jnp.zeros_like(st_ref)

    lane1 = lax.broadcasted_iota(jnp.int32, (1, LANES), 1)
    avec = jnp.where(lane1 < 2 * SSD_HEADS, -jnp.exp(alog_ref[...]), 0.0)
    row = lax.broadcasted_iota(jnp.int32, (q, q), 0)
    col = lax.broadcasted_iota(jnp.int32, (q, q), 1)
    masks = (col <= row, col >= row)
    lo = _lane_iota(q) < SSD_HEAD_DIM
    lo1 = lane1 < SSD_HEAD_DIM
    streams = [(d, g) for d in range(2) for g in range(SSD_GROUPS)]

    dts, a_cums, a_cum_ts, a_tots = [], [], [], []
    for d in range(2):
        dt = dt_refs[d][0]
        dta = dt * avec
        ones = masks[d].astype(BF16)
        hi = dta.astype(BF16)
        r1 = dta - hi.astype(F32)
        mid = r1.astype(BF16)
        lo3 = (r1 - mid.astype(F32)).astype(BF16)
        a_cum = _dot(ones, hi) + _dot(ones, mid) + _dot(ones, lo3)
        dts.append(dt)
        a_cums.append(a_cum)
        a_cum_ts.append(a_cum.T)
        a_tots.append(a_cum[q - 1:q, :] if d == 0 else a_cum[0:1, :])

    def head_col(arr, d, hh):
        c = SSD_HEADS * d + hh
        return arr[:, c:c + 1]

    bgs, cgs, gmats, xdts, acls, atls = {}, {}, {}, {}, {}, {}
    for d, g in streams:
        u_ref = u_refs[d]
        bgs[d, g] = u_ref[0, :, SSD_INNER + g * SSD_STATE:SSD_INNER + (g + 1) * SSD_STATE]
        cgs[d, g] = u_ref[0, :, SSD_INNER + (SSD_GROUPS + g) * SSD_STATE:SSD_INNER + (SSD_GROUPS + g + 1) * SSD_STATE]
        gmats[d, g] = _dot_nt(cgs[d, g], bgs[d, g])
    for d, g in streams:
        h0, h1 = 2 * g, 2 * g + 1
        xs = u_refs[d][0, :, g * LANES:(g + 1) * LANES].astype(F32)
        acls[d, g] = jnp.where(lo, head_col(a_cums[d], d, h0), head_col(a_cums[d], d, h1))
        atls[d, g] = jnp.where(lo1, head_col(a_tots[d], d, h0), head_col(a_tots[d], d, h1))
        xdts[d, g] = xs * jnp.where(lo, head_col(dts[d], d, h0), head_col(dts[d], d, h1))

    intra = {}
    for d, g in streams:
        xdt_b = xdts[d, g].astype(BF16)
        ys = []
        for hh in (2 * g, 2 * g + 1):
            c = SSD_HEADS * d + hh
            seg = a_cums[d][:, c:c + 1] - a_cum_ts[d][c:c + 1, :]
            dec = jnp.exp(jnp.where(masks[d], seg, NEG_BIG))
            ys.append(_dot((gmats[d, g] * dec).astype(BF16), xdt_b))
        intra[d, g] = jnp.where(lo, ys[0], ys[1])

    for d, g in streams:
        st = st_ref[d, g]
        y = intra[d, g] + _dot(cgs[d, g], st.astype(BF16)) * jnp.exp(acls[d, g])
        y_refs[d][0, :, g * LANES:(g + 1) * LANES] = y.astype(y_refs[d].dtype)
        w = (xdts[d, g] * jnp.exp(atls[d, g] - acls[d, g])).astype(BF16)
        st_ref[d, g] = st * jnp.exp(atls[d, g]) + _dot_tn(bgs[d, g], w)


def _ssd(u, dt, alog_row, n_lat):
    bsz, t, c = u.shape
    nc = t // SSD_Q
    ncl = n_lat // SSD_Q
    ncc = nc - ncl

    def fwd(s):
        return jnp.where(s < ncc, ncl + s, s - ncc)

    def bwd(s):
        return jnp.where(s < ncc, nc - 1 - s, ncl - 1 - (s - ncc))

    y_shape = jax.ShapeDtypeStruct((bsz, t, SSD_INNER), BF16)
    return pl.pallas_call(
        _ssd_kernel,
        grid=(bsz, nc),
        in_specs=[pl.BlockSpec((1, SSD_Q, c), lambda b, s: (b, fwd(s), 0)),
                  pl.BlockSpec((1, SSD_Q, c), lambda b, s: (b, bwd(s), 0)),
                  pl.BlockSpec((1, SSD_Q, LANES), lambda b, s: (b, fwd(s), 0)),
                  pl.BlockSpec((1, SSD_Q, LANES), lambda b, s: (b, bwd(s), 0)),
                  pl.BlockSpec((1, LANES), lambda b, s: (0, 0))],
        out_specs=[pl.BlockSpec((1, SSD_Q, SSD_INNER), lambda b, s: (b, fwd(s), 0)),
                   pl.BlockSpec((1, SSD_Q, SSD_INNER), lambda b, s: (b, bwd(s), 0))],
        out_shape=[y_shape, y_shape],
        scratch_shapes=[pltpu.VMEM((2, SSD_GROUPS, SSD_STATE, LANES), F32)],
        compiler_params=_cparams(("parallel", "arbitrary")),
        name="ssd_scan",
    )(u, u, dt, dt, alog_row)


def _outproj_kernel(oab_ref, oc_ref, yf_ref, yb_ref, u_ref, g_ref, x_ref, mod_ref, pv_ref, w_ref, fw_ref, o_ref,
                    *, final):
    g = g_ref[0].astype(F32)
    oab = oab_ref[0].astype(F32)
    m_a = oab[:, 0:256] * g[:, 0:256]
    m_b = oab[:, 256:512] * g[:, 256:512]
    m_c = oc_ref[0].astype(F32) * g[:, 512:768]
    yd = pv_ref[0:1, :] * u_ref[0].astype(F32) + yf_ref[0].astype(F32) + yb_ref[0].astype(F32)
    td = yd * g[:, 768:1024]
    m_d = td * _rms(td, SSD_INNER) * pv_ref[1:2, :]
    acc = _dot(m_a.astype(BF16), w_ref[0:256, :])
    acc += _dot(m_b.astype(BF16), w_ref[256:512, :])
    acc += _dot(m_c.astype(BF16), w_ref[512:768, :])
    acc += _dot(m_d.astype(BF16), w_ref[768:1024, :])
    xn = x_ref[0] + mod_ref[0, 2:3, :] * acc
    if final:
        xn = xn * _rms(xn, xn.shape[-1]) * fw_ref[...]
    o_ref[0] = xn


def _outproj(oab, oc, yf, yb, u, gates, xall, mods, pvec, wout, fw, n_lat, final):
    bsz, t, d = xall.shape
    nlt = n_lat // TM
    rows = n_lat if final else t
    tok = lambda w: pl.BlockSpec((1, TM, w), lambda b, i: (b, i, 0))
    full = lambda a: pl.BlockSpec(a.shape, lambda b, i: (0,) * a.ndim)
    return pl.pallas_call(
        functools.partial(_outproj_kernel, final=final),
        grid=(bsz, rows // TM),
        in_specs=[tok(4 * LANES), tok(2 * LANES), tok(SSD_INNER), tok(SSD_INNER),
                  tok(SSD_INNER), tok(1024), tok(d),
                  pl.BlockSpec((1, 3, d), lambda b, i: (jnp.where(i < nlt, b, bsz), 0, 0)),
                  full(pvec), full(wout), full(fw)],
        out_specs=tok(d),
        out_shape=jax.ShapeDtypeStruct((bsz, rows, d), F32),
        compiler_params=_cparams(("parallel", "parallel")),
        name="outproj",
    )(oab, oc, yf, yb, u, gates, xall, mods, pvec, wout, fw)


def _rope_tables(n_ctx, n_lat):
    rows = n_lat // GRID_W
    row = jnp.repeat(jnp.arange(rows, dtype=F32), GRID_W)
    col = jnp.tile(jnp.arange(GRID_W, dtype=F32), rows)

    def cos_sin(rot_dim):
        n_freq = rot_dim // 4
        inv_freq = ROPE_THETA ** (-jnp.arange(n_freq, dtype=F32) / n_freq)
        ang = jnp.concatenate([row[:, None] * inv_freq, col[:, None] * inv_freq], axis=-1)
        return jnp.cos(ang), jnp.sin(ang)

    def with_ctx(tab, fill):
        return jnp.concatenate([tab, jnp.full((n_ctx, LANES), fill, F32)], axis=0)

    lane = np.arange(LANES)
    cos_a, sin_a = cos_sin(MLA_ROPE)
    half = MLA_ROPE // 2
    idx = np.where((lane >= MLA_NOPE) & (lane < MLA_NOPE + MLA_ROPE), (lane - MLA_NOPE) % half, 0)
    first = (lane >= MLA_NOPE) & (lane < MLA_NOPE + half)
    second = (lane >= MLA_NOPE + half) & (lane < MLA_NOPE + MLA_ROPE)
    rot = first | second
    ca = jnp.where(rot, cos_a[:, idx], 1.0)
    sa_up = jnp.where(first, -sin_a[:, idx], 0.0)
    sa_dn = jnp.where(second, sin_a[:, idx], 0.0)
    cos_h, sin_h = cos_sin(HEAD_DIM)
    half = HEAD_DIM // 2
    idx = lane % half
    first = (lane % HEAD_DIM) < half
    cb = cos_h[:, idx]
    sb_up = jnp.where(first, -sin_h[:, idx], 0.0)
    sb_dn = jnp.where(first, 0.0, sin_h[:, idx])
    return jnp.stack([with_ctx(ca, 1.0), with_ctx(sa_up, 0.0), with_ctx(sa_dn, 0.0),
                      with_ctx(cb, 1.0), with_ctx(sb_up, 0.0), with_ctx(sb_dn, 0.0)])


def _relayout_w_in(w_in):
    depth, d, _ = w_in.shape
    w = w_in.astype(BF16)
    kr0 = MLA_Q_RANK + MLA_KV_RANK
    dt0 = w_in.shape[-1] - 2 * SSD_HEADS
    z = lambda n: jnp.zeros((depth, d, n), BF16)
    cat = jnp.concatenate([w[..., :kr0], z(MLA_NOPE), w[..., kr0:kr0 + MLA_ROPE], z(LANES - MLA_NOPE - MLA_ROPE),
                           w[..., kr0 + MLA_ROPE:dt0], w[..., dt0:], z(LANES - 2 * SSD_HEADS)], axis=-1)
    assert cat.shape[-1] == C_END
    return cat


def _relayout_mla(w_q_up, w_kv_up):
    depth = w_q_up.shape[0]
    wq = w_q_up.reshape(depth, MLA_Q_RANK, MLA_HEADS, MLA_NOPE + MLA_ROPE)
    wq = jnp.pad(wq, ((0, 0), (0, 0), (0, 0), (0, LANES - MLA_NOPE - MLA_ROPE)))
    wq = wq.reshape(depth, MLA_Q_RANK, MLA_HEADS * LANES)
    wkv = w_kv_up.reshape(depth, MLA_KV_RANK, MLA_HEADS, MLA_NOPE + MLA_V)
    wk = jnp.pad(wkv[..., :MLA_NOPE], ((0, 0), (0, 0), (0, 0), (0, LANES - MLA_NOPE)))
    wk = wk.reshape(depth, MLA_KV_RANK, MLA_HEADS * LANES)
    wv = wkv[..., MLA_NOPE:].reshape(depth, MLA_KV_RANK, MLA_HEADS * MLA_V)
    return wq.astype(BF16), jnp.concatenate([wk, wv], axis=-1).astype(BF16)


def _pad_lanes(v, width):
    return jnp.pad(v, ((0, 0), (0, width - v.shape[-1])))


def kernel(x, c, ctx, c_ctx, norm_w, w_mod, b_mod, w_in, mla_q_norm, mla_kv_norm, mla_w_q_up, mla_w_kv_up,
           gqa_q_norm, gqa_k_norm, swa_sink, ssd_conv_w, ssd_conv_b, ssd_a_log, ssd_dt_bias, ssd_d,
           ssd_norm_w, w_out, final_norm_w):
    bsz, n_lat, d = x.shape
    n_ctx = ctx.shape[1]
    depth = w_in.shape[0]
    assert d == 1024 and bsz % NB_IN == 0 and bsz + NB_IN <= 8
    assert n_ctx % TM == 0 and n_ctx % TQ_CTX == 0 and n_ctx % TQ_SWA == 0 and n_ctx % SSD_Q == 0
    assert n_lat % TQ == 0 and n_lat % n_ctx == 0 and n_lat % GRID_W == 0
    assert n_lat >= TQ_SWA + 2 * WINDOW

    xall = jnp.concatenate([x, ctx], axis=1)
    cvec = jnp.concatenate([c, jnp.broadcast_to(c_ctx, (NB_IN, d)), jnp.zeros((8 - bsz - NB_IN, d), F32)], axis=0)
    mods = _modulation(cvec, w_mod, b_mod).reshape(depth, 8, 3, d)[:, :bsz + NB_IN]

    wcat = _relayout_w_in(w_in)
    wq, wkv = _relayout_mla(mla_w_q_up, mla_w_kv_up)
    wout = w_out.astype(BF16)
    ropes = _rope_tables(n_ctx, n_lat)
    pv_in = jnp.stack([mla_q_norm,
                       _pad_lanes(mla_kv_norm, 256),
                       jnp.tile(gqa_q_norm, (1, 4)),
                       _pad_lanes(jnp.tile(gqa_k_norm, (1, 2)), 256),
                       _pad_lanes(ssd_dt_bias.reshape(depth, -1), 256)], axis=1)
    pv_in = jnp.pad(pv_in, ((0, 0), (0, 3), (0, 0)))
    pv_out = jnp.stack([jnp.repeat(ssd_d, SSD_HEAD_DIM, axis=-1), ssd_norm_w], axis=1)
    pv_out = jnp.pad(pv_out, ((0, 0), (0, 6), (0, 0)))
    sink_rows = jnp.broadcast_to(jnp.pad(swa_sink, ((0, 0), (0, 4)))[:, :, None], (depth, 8, LANES))
    conv_w8 = jnp.pad(ssd_conv_w, ((0, 0), (0, 8 - SSD_CONV), (0, 0)))
    alog = _pad_lanes(ssd_a_log.reshape(depth, -1), LANES)
    fw = final_norm_w.reshape(1, d)

    for l in range(depth):
        final = l == depth - 1
        qall, kall, vall, swa_in, gates, xbc, dt = _inproj(
            xall, mods[l], norm_w[l].reshape(1, d), wcat[l], wq[l], wkv[l], pv_in[l], ropes, n_lat)
        oab = _attention_latent(qall, kall, vall, n_lat)
        if not final:
            oab = _attention_ctx(qall, kall, vall, oab, n_lat, n_ctx)
        oc = _swa(swa_in, sink_rows[l], n_lat, n_ctx)
        u = _conv(xbc, conv_w8[l], ssd_conv_b[l].reshape(1, -1), n_lat)
        yf, yb = _ssd(u, dt, alog[l].reshape(1, LANES), n_lat)
        xall = _outproj(oab, oc, yf, yb, u, gates, xall, mods[l], pv_out[l], wout[l], fw, n_lat, final)
    return xall
```

```python
import functools

import numpy as np
import jax
import jax.numpy as jnp
from jax import lax
from jax.experimental import pallas as pl
from jax.experimental.pallas import tpu as pltpu

F32 = jnp.float32
BF16 = jnp.bfloat16

EPS = 1e-6
GRID_W = 64
ROPE_THETA = 10000.0
HEAD_DIM = 64
MLA_HEADS, MLA_NOPE, MLA_ROPE, MLA_V = 4, 64, 32, 64
MLA_Q_RANK, MLA_KV_RANK = 256, 128
GQA_HEADS, GQA_KV_HEADS = 4, 2
SWA_HEADS, SWA_KV_HEADS = 4, 2
WINDOW = 128
SSD_HEADS, SSD_HEAD_DIM, SSD_GROUPS, SSD_STATE, SSD_CONV = 4, 64, 2, 128, 5
SSD_INNER = SSD_HEADS * SSD_HEAD_DIM
SSD_CONV_DIM = SSD_INNER + 2 * SSD_GROUPS * SSD_STATE

LANES = 128
LOG2E = 1.4426950408889634
NEG_BIG = -1e30

C_CQ, C_CKV, C_KR, C_GA = 0, 256, 384, 512
C_QB, C_KB, C_VB, C_GB = 768, 1024, 1152, 1280
C_QC, C_KC, C_VC, C_GC = 1536, 1792, 1920, 2048
C_Z, C_XBC, C_END = 2304, 2560, 3328
C_DT = C_KR

TM = 256
NB_IN = 2
TQ = 512
TQ_CTX = 256
KV_CHUNK = 1024
TQ_SWA = 256
SSD_Q = 256
HALO = 8
VMEM_LIMIT = 56 * 1024 * 1024


def _cparams(sem):
    return pltpu.CompilerParams(dimension_semantics=sem, vmem_limit_bytes=VMEM_LIMIT)


def _sigmoid(x):
    return 1.0 / (1.0 + jnp.exp(-x))


def _silu(x):
    return x * _sigmoid(x)


def _rms(x, n):
    return lax.rsqrt(jnp.sum(x * x, axis=-1, keepdims=True) * (1.0 / n) + EPS)


def _dot(a, b):
    return jnp.dot(a, b, preferred_element_type=F32)


def _dot_nt(a, b):
    return lax.dot_general(a, b, (((1,), (1,)), ((), ())), preferred_element_type=F32)


def _dot_tn(a, b):
    return lax.dot_general(a, b, (((0,), (0,)), ((), ())), preferred_element_type=F32)


def _lane_iota(rows):
    return lax.broadcasted_iota(jnp.int32, (rows, LANES), 1)


def _mod_kernel(c_ref, w_ref, b_ref, o_ref):
    cs = _silu(c_ref[...])
    o_ref[0] = _dot(cs.astype(BF16), w_ref[0].astype(BF16)) + b_ref[0]


def _modulation(cvec, w_mod, b_mod):
    depth, d, d3 = w_mod.shape
    tn = 512
    return pl.pallas_call(
        _mod_kernel,
        grid=(depth, d3 // tn),
        in_specs=[pl.BlockSpec((8, d), lambda l, j: (0, 0)),
                  pl.BlockSpec((1, d, tn), lambda l, j: (l, 0, j)),
                  pl.BlockSpec((1, 1, tn), lambda l, j: (l, 0, j))],
        out_specs=pl.BlockSpec((1, 8, tn), lambda l, j: (l, 0, j)),
        out_shape=jax.ShapeDtypeStruct((depth, 8, d3), F32),
        compiler_params=_cparams(("parallel", "parallel")),
        name="modulation",
    )(cvec, w_mod, b_mod.reshape(depth, 1, d3))


def _rope(x, c, s_up, s_dn, half):
    return x * c + pltpu.roll(x, LANES - half, 1) * s_up + pltpu.roll(x, half, 1) * s_dn


def _head_rms(blk, lane):
    sq = blk * blk
    lo = lane < HEAD_DIM
    s_lo = jnp.sum(jnp.where(lo, sq, 0.0), axis=-1, keepdims=True)
    s_hi = jnp.sum(jnp.where(lo, 0.0, sq), axis=-1, keepdims=True)
    r_lo = lax.rsqrt(s_lo * (1.0 / HEAD_DIM) + EPS)
    r_hi = lax.rsqrt(s_hi * (1.0 / HEAD_DIM) + EPS)
    return jnp.where(lo, r_lo, r_hi)


def _dup_halves(blk, lane):
    sw = pltpu.roll(blk, HEAD_DIM, 1)
    lo = lane < HEAD_DIM
    return jnp.where(lo, blk, sw), jnp.where(lo, sw, blk)


def _store_v_slots(v_ref, bb, slot, pair, lo):
    v_ref[bb, :, slot * LANES:(slot + 1) * LANES] = jnp.where(lo, pair, 1.0).astype(BF16)
    v_ref[bb, :, (slot + 1) * LANES:(slot + 2) * LANES] = jnp.where(lo, 1.0, pair).astype(BF16)


def _inproj_kernel(*refs, lat_tiles, split_x):
    refs = list(refs)
    x_ref = refs.pop(0)
    xctx_ref = refs.pop(0) if split_x else None
    (mod_ref, nw_ref, wcat_ref, wq_ref, wkv_ref, pv_ref, rope_ref,
     qall_ref, kall_ref, vall_ref, swa_ref, g_ref, xbc_ref, dt_ref) = refs
    nb, tm, d = x_ref.shape
    lane = _lane_iota(tm)
    lo = lane < HEAD_DIM
    is_ctx = pl.program_id(1) >= lat_tiles

    hs = []
    for bb in range(nb):
        x = jnp.where(is_ctx, xctx_ref[bb], x_ref[bb]) if split_x else x_ref[bb]
        xn = x * _rms(x, d)
        h = (xn * nw_ref[...]) * (1.0 + mod_ref[bb, 1:2, :]) + mod_ref[bb, 0:1, :]
        hs.append(h.astype(BF16))
    hcat = jnp.concatenate(hs, axis=0)

    bounds = (C_CQ, C_QB, C_QC, C_Z, C_END)
    proj = {}

    def project(gi):
        proj[gi] = _dot(hcat, wcat_ref[:, bounds[gi]:bounds[gi + 1]])

    def cols(bb, c0, width):
        gi = max(i for i in range(4) if bounds[i] <= c0)
        off = c0 - bounds[gi]
        rows = slice(None) if bb is None else slice(bb * tm, (bb + 1) * tm)
        return proj[gi][rows, off:off + width]

    ca, sa_up, sa_dn = rope_ref[0], rope_ref[1], rope_ref[2]
    cb, sb_up, sb_dn = rope_ref[3], rope_ref[4], rope_ref[5]
    qscale_a = (MLA_NOPE + MLA_ROPE) ** -0.5 * LOG2E
    qscale_h = HEAD_DIM ** -0.5 * LOG2E
    base = MLA_HEADS * LANES

    def stage_a():
        cq = cols(None, C_CQ, MLA_Q_RANK)
        cqn = cq * _rms(cq, MLA_Q_RANK) * pv_ref[0:1, :]
        qa_all = _dot(cqn.astype(BF16), wq_ref[...])
        ckv = cols(None, C_CKV, MLA_KV_RANK)
        ckvn = ckv * _rms(ckv, MLA_KV_RANK) * pv_ref[1:2, 0:MLA_KV_RANK]
        kv_all = _dot(ckvn.astype(BF16), wkv_ref[...])
        for bb in range(nb):
            rows = slice(bb * tm, (bb + 1) * tm)
            qa, kv = qa_all[rows], kv_all[rows]
            kr = jnp.where(lane >= MLA_NOPE, _rope(cols(bb, C_KR, LANES), ca, sa_up, sa_dn, MLA_ROPE // 2), 0.0)
            for hh in range(MLA_HEADS):
                sl = slice(hh * LANES, (hh + 1) * LANES)
                qall_ref[bb, :, sl] = (_rope(qa[:, sl], ca, sa_up, sa_dn, MLA_ROPE // 2) * qscale_a).astype(BF16)
                kall_ref[bb, :, sl] = (kv[:, sl] + kr).astype(BF16)
            for pr in range(2):
                _store_v_slots(vall_ref, bb, 2 * pr, kv[:, (4 + pr) * LANES:(5 + pr) * LANES], lo)
            g_ref[bb, :, 0:256] = _silu(cols(bb, C_GA, 256)).astype(BF16)

    def stage_b(bb):
        for blk_i in range(2):
            qb = cols(bb, C_QB + blk_i * LANES, LANES)
            qb = qb * _head_rms(qb, lane) * pv_ref[2:3, 0:LANES]
            qb = _rope(qb, cb, sb_up, sb_dn, HEAD_DIM // 2) * qscale_h
            c0 = base + 2 * blk_i * LANES
            qall_ref[bb, :, c0:c0 + LANES] = jnp.where(lo, qb, 0.0).astype(BF16)
            qall_ref[bb, :, c0 + LANES:c0 + 2 * LANES] = jnp.where(lo, 0.0, qb).astype(BF16)
        kb = cols(bb, C_KB, LANES)
        kb = kb * _head_rms(kb, lane) * pv_ref[3:4, 0:LANES]
        kb = _rope(kb, cb, sb_up, sb_dn, HEAD_DIM // 2)
        k0, k1 = _dup_halves(kb, lane)
        k0 = k0.astype(BF16)
        k1 = k1.astype(BF16)
        kall_ref[bb, :, base:base + LANES] = k0
        kall_ref[bb, :, base + LANES:base + 2 * LANES] = k0
        kall_ref[bb, :, base + 2 * LANES:base + 3 * LANES] = k1
        kall_ref[bb, :, base + 3 * LANES:base + 4 * LANES] = k1
        v0, v1 = _dup_halves(cols(bb, C_VB, LANES), lane)
        _store_v_slots(vall_ref, bb, 4, v0, lo)
        _store_v_slots(vall_ref, bb, 6, v1, lo)
        g_ref[bb, :, 256:512] = _silu(cols(bb, C_GB, 256)).astype(BF16)

    def stage_c(bb):
        for blk_i in range(2):
            qc = cols(bb, C_QC + blk_i * LANES, LANES)
            swa_ref[bb, :, blk_i * LANES:(blk_i + 1) * LANES] = (
                _rope(qc, cb, sb_up, sb_dn, HEAD_DIM // 2) * qscale_h).astype(BF16)
        kc = _rope(cols(bb, C_KC, LANES), cb, sb_up, sb_dn, HEAD_DIM // 2)
        kc0, kc1 = _dup_halves(kc, lane)
        swa_ref[bb, :, 2 * LANES:3 * LANES] = kc0.astype(BF16)
        swa_ref[bb, :, 3 * LANES:4 * LANES] = kc1.astype(BF16)
        vc0, vc1 = _dup_halves(cols(bb, C_VC, LANES), lane)
        _store_v_slots(swa_ref, bb, 4, vc0, lo)
        _store_v_slots(swa_ref, bb, 6, vc1, lo)
        g_ref[bb, :, 512:768] = _silu(cols(bb, C_GC, 256)).astype(BF16)

    def stage_d(bb):
        g_ref[bb, :, 768:1024] = _silu(cols(bb, C_Z, 256)).astype(BF16)
        xbc_ref[bb] = cols(bb, C_XBC, SSD_CONV_DIM)
        dtr = cols(bb, C_DT, LANES) + pv_ref[4:5, 0:LANES]
        dt_ref[bb] = jnp.maximum(dtr, 0.0) + jnp.log(1.0 + jnp.exp(-jnp.abs(dtr)))

    project(0)
    project(1)
    stage_a()
    project(2)
    for bb in range(nb):
        stage_b(bb)
    project(3)
    for bb in range(nb):
        stage_c(bb)
    for bb in range(nb):
        stage_d(bb)


def _inproj(xsrc, mods, nw, wcat, wq, wkv, pvec, ropes, layer, n_lat):
    bsz, _, d = xsrc[0].shape
    t = ropes.shape[1]
    nlt = n_lat // TM
    split_x = len(xsrc) == 2
    grid = (bsz // NB_IN, t // TM)
    tok = lambda w: pl.BlockSpec((NB_IN, TM, w), lambda b, i: (b, i, 0))
    layer_blk = lambda a: pl.BlockSpec((None,) + a.shape[1:], lambda b, i: (layer,) + (0,) * (a.ndim - 1))
    x_specs = [tok(d)]
    if split_x:
        x_specs = [pl.BlockSpec((NB_IN, TM, d), lambda b, i: (b, jnp.minimum(i, nlt - 1), 0)),
                   pl.BlockSpec((NB_IN, TM, d), lambda b, i: (b, jnp.maximum(i - nlt, 0), 0))]
    widths = (8 * LANES, 8 * LANES, 8 * LANES, 8 * LANES, 1024, SSD_CONV_DIM, LANES)
    dtypes = (BF16, BF16, BF16, BF16, BF16, F32, F32)
    return pl.pallas_call(
        functools.partial(_inproj_kernel, lat_tiles=nlt, split_x=split_x),
        grid=grid,
        in_specs=x_specs + [
            pl.BlockSpec((None, NB_IN, 3, d), lambda b, i: (layer, jnp.where(i < nlt, b, bsz // NB_IN), 0, 0)),
            layer_blk(nw), layer_blk(wcat), layer_blk(wq), layer_blk(wkv), layer_blk(pvec),
            pl.BlockSpec((6, TM, LANES), lambda b, i: (0, i, 0))],
        out_specs=[tok(w) for w in widths],
        out_shape=[jax.ShapeDtypeStruct((bsz, t, w), dt) for w, dt in zip(widths, dtypes)],
        compiler_params=_cparams(("parallel", "parallel")),
        name="inproj",
    )(*xsrc, mods, nw, wcat, wq, wkv, pvec, ropes)


def _kv_chunks(nk):
    return [(c0, min(KV_CHUNK, nk - c0)) for c0 in range(0, nk, KV_CHUNK)]


def _logits_pass(q, k_ref, s_ref, sl, nk):
    mrun = None
    for c0, cw in _kv_chunks(nk):
        s = _dot_nt(q, k_ref[0, c0:c0 + cw, sl])
        s_ref[:, c0:c0 + cw] = s
        for t0 in range(0, cw, LANES):
            piece = s[:, t0:t0 + LANES]
            mrun = piece if mrun is None else jnp.maximum(mrun, piece)
    return jnp.max(mrun, axis=-1, keepdims=True)


def _pv_pass(m, v_ref, s_ref, sl, nk):
    acc = None
    for c0, cw in _kv_chunks(nk):
        pr = jnp.exp2(s_ref[:, c0:c0 + cw] - m).astype(BF16)
        o = _dot(pr, v_ref[0, c0:c0 + cw, sl])
        acc = o if acc is None else acc + o
    return acc / pltpu.roll(acc, HEAD_DIM, 1)


def _attn_body(q_ref, k_ref, v_ref, o_ref, s0_ref, s1_ref):
    tq = q_ref.shape[1]
    nk = k_ref.shape[1]
    lane = _lane_iota(tq)
    sls = (slice(0, LANES), slice(LANES, 2 * LANES))
    s_refs = (s0_ref, s1_ref)
    ms = [_logits_pass(q_ref[0, :, sls[j]], k_ref, s_refs[j], sls[j], nk) for j in range(2)]
    outs = [_pv_pass(ms[j], v_ref, s_refs[j], sls[j], nk) for j in range(2)]
    o_ref[0] = jnp.where(lane < HEAD_DIM, outs[0], outs[1]).astype(o_ref.dtype)


def _attention_latent(qall, kall, vall, n_lat):
    bsz, t, _ = qall.shape
    return pl.pallas_call(
        _attn_body,
        grid=(bsz, 4, n_lat // TQ),
        in_specs=[pl.BlockSpec((1, TQ, 2 * LANES), lambda b, p, i: (b, i, p)),
                  pl.BlockSpec((1, t, 2 * LANES), lambda b, p, i: (b, 0, p)),
                  pl.BlockSpec((1, t, 2 * LANES), lambda b, p, i: (b, 0, p))],
        out_specs=pl.BlockSpec((1, TQ, LANES), lambda b, p, i: (b, i, p)),
        out_shape=jax.ShapeDtypeStruct((bsz, n_lat, 4 * LANES), BF16),
        scratch_shapes=[pltpu.VMEM((TQ, t), F32), pltpu.VMEM((TQ, t), F32)],
        compiler_params=_cparams(("parallel", "parallel", "arbitrary")),
        name="attn_global",
    )(qall, kall, vall)


def _attention_ctx(qall, kall, vall, n_lat, n_ctx):
    bsz, t, _ = qall.shape
    qoff = n_lat // TQ_CTX
    koff = n_lat // n_ctx
    return pl.pallas_call(
        _attn_body,
        grid=(bsz, 4, n_ctx // TQ_CTX),
        in_specs=[pl.BlockSpec((1, TQ_CTX, 2 * LANES), lambda b, p, i: (b, qoff + i, p)),
                  pl.BlockSpec((1, n_ctx, 2 * LANES), lambda b, p, i: (b, koff, p)),
                  pl.BlockSpec((1, n_ctx, 2 * LANES), lambda b, p, i: (b, koff, p))],
        out_specs=pl.BlockSpec((1, TQ_CTX, LANES), lambda b, p, i: (b, i, p)),
        out_shape=jax.ShapeDtypeStruct((bsz, n_ctx, 4 * LANES), BF16),
        scratch_shapes=[pltpu.VMEM((TQ_CTX, n_ctx), F32), pltpu.VMEM((TQ_CTX, n_ctx), F32)],
        compiler_params=_cparams(("parallel", "parallel", "arbitrary")),
        name="attn_global_ctx",
    )(qall, kall, vall)


def _swa_kernel(q_ref, k_ref, v_ref, bias_ref, sink_ref, o_ref, *, n_lat, n_ctx):
    tq = q_ref.shape[1]
    band = tq + 2 * WINDOW
    i = pl.program_id(1)
    lane = _lane_iota(tq)
    lo = lane < HEAD_DIM

    start = pl.multiple_of(jnp.clip(i * tq - WINDOW, 0, n_lat - band), WINDOW)
    bias = bias_ref[0]
    k = jnp.concatenate([k_ref[0, n_lat:n_lat + n_ctx, :], k_ref[0, pl.ds(start, band), :]], axis=0)
    v = jnp.concatenate([v_ref[0, n_lat:n_lat + n_ctx, :], v_ref[0, pl.ds(start, band), :]], axis=0)

    def logits_of(hh):
        ps = slice((hh // 2) * LANES, (hh // 2 + 1) * LANES)
        q = q_ref[0, :, ps]
        zero = jnp.zeros_like(q)
        qj = jnp.where(lo, q, zero) if hh % 2 == 0 else jnp.where(lo, zero, q)
        s = _dot_nt(qj, k[:, ps]) + bias
        sink = sink_ref[hh:hh + 1, 0:1] * LOG2E
        return s, sink, jnp.maximum(jnp.max(s, axis=-1, keepdims=True), sink)

    ahead = 2
    logits = [logits_of(hh) for hh in range(ahead)]
    outs = []
    for hh in range(SWA_HEADS):
        if hh + ahead < SWA_HEADS:
            logits.append(logits_of(hh + ahead))
        s, sink, m = logits[hh]
        acc = _dot(jnp.exp2(s - m).astype(BF16), v[:, hh * LANES:(hh + 1) * LANES])
        outs.append(acc / (pltpu.roll(acc, HEAD_DIM, 1) + jnp.exp2(sink - m)))
    for pair in range(2):
        o_ref[0, :, pair * LANES:(pair + 1) * LANES] = jnp.where(
            lo, outs[2 * pair], outs[2 * pair + 1]).astype(o_ref.dtype)


def _swa_bias(n_ctx):
    band = TQ_SWA + 2 * WINDOW
    r = np.arange(TQ_SWA)[:, None]
    c = np.arange(band)[None, :]
    ok = [np.abs(c - shift - r) <= WINDOW for shift in (0, WINDOW, 2 * WINDOW)] + [np.zeros((TQ_SWA, band), bool)]
    lat = np.where(np.stack(ok), 0.0, NEG_BIG).astype(np.float32)
    return jnp.asarray(np.concatenate([np.zeros((4, TQ_SWA, n_ctx), np.float32), lat], axis=-1))


def _swa(swa_in, sink_rows, layer, n_lat, n_ctx):
    bsz, t, _ = swa_in.shape
    nlt = n_lat // TQ_SWA
    bias = _swa_bias(n_ctx)

    def kind(i):
        return jnp.where(i >= nlt, 3, jnp.where(i == 0, 0, jnp.where(i == nlt - 1, 2, 1)))

    return pl.pallas_call(
        functools.partial(_swa_kernel, n_lat=n_lat, n_ctx=n_ctx),
        grid=(bsz, t // TQ_SWA),
        in_specs=[pl.BlockSpec((1, TQ_SWA, 2 * LANES), lambda b, i: (b, i, 0)),
                  pl.BlockSpec((1, t, 2 * LANES), lambda b, i: (b, 0, 1)),
                  pl.BlockSpec((1, t, 4 * LANES), lambda b, i: (b, 0, 1)),
                  pl.BlockSpec((1,) + bias.shape[1:], lambda b, i: (kind(i), 0, 0)),
                  pl.BlockSpec((None,) + sink_rows.shape[1:], lambda b, i: (layer, 0, 0))],
        out_specs=pl.BlockSpec((1, TQ_SWA, 2 * LANES), lambda b, i: (b, i, 0)),
        out_shape=jax.ShapeDtypeStruct((bsz, t, 2 * LANES), BF16),
        compiler_params=_cparams(("parallel", "arbitrary")),
        name="attn_window",
    )(swa_in, swa_in, swa_in, bias, sink_rows)


def _conv_kernel(x_ref, prev_ref, next_ref, w_ref, b_ref, u_ref, *, lat_tiles):
    tm = x_ref.shape[1]
    i = pl.program_id(1)
    first = (i == 0) | (i == lat_tiles)
    last = (i == lat_tiles - 1) | (i == pl.num_programs(1) - 1)
    ext = jnp.concatenate([jnp.where(first, 0.0, prev_ref[0]), x_ref[0], jnp.where(last, 0.0, next_ref[0])], axis=0)
    n_ext = tm + 2 * HALO
    acc = x_ref[0] * w_ref[SSD_CONV // 2:SSD_CONV // 2 + 1, :] + b_ref[...]
    for tap in range(SSD_CONV):
        if tap != SSD_CONV // 2:
            shifted = pltpu.roll(ext, (SSD_CONV // 2 - tap) % n_ext, 0)
            acc = acc + shifted[HALO:HALO + tm, :] * w_ref[tap:tap + 1, :]
    u_ref[0] = _silu(acc).astype(u_ref.dtype)


def _conv(xbc, conv_w8, conv_b, layer, n_lat):
    bsz, t, c = xbc.shape
    per = TM // HALO
    nh = t // HALO
    return pl.pallas_call(
        functools.partial(_conv_kernel, lat_tiles=n_lat // TM),
        grid=(bsz, t // TM),
        in_specs=[pl.BlockSpec((1, TM, c), lambda b, i: (b, i, 0)),
                  pl.BlockSpec((1, HALO, c), lambda b, i: (b, jnp.maximum(i * per - 1, 0), 0)),
                  pl.BlockSpec((1, HALO, c), lambda b, i: (b, jnp.minimum((i + 1) * per, nh - 1), 0)),
                  pl.BlockSpec((None,) + conv_w8.shape[1:], lambda b, i: (layer, 0, 0)),
                  pl.BlockSpec((None,) + conv_b.shape[1:], lambda b, i: (layer, 0, 0))],
        out_specs=pl.BlockSpec((1, TM, c), lambda b, i: (b, i, 0)),
        out_shape=jax.ShapeDtypeStruct((bsz, t, c), BF16),
        compiler_params=_cparams(("parallel", "parallel")),
        name="ssd_conv",
    )(xbc, xbc, xbc, conv_w8, conv_b)


def _ssd_kernel(uf_ref, ub_ref, dtf_ref, dtb_ref, alog_ref, pv_ref, yf_ref, yb_ref, st_ref):
    q = uf_ref.shape[1]
    dt_refs, y_refs = (dtf_ref, dtb_ref), (yf_ref, yb_ref)
    us = (uf_ref[0], ub_ref[0])

    @pl.when(pl.program_id(1) == 0)
    def _():
        st_ref[...] = jnp.zeros_like(st_ref)

    lane1 = lax.broadcasted_iota(jnp.int32, (1, LANES), 1)
    avec = jnp.where(lane1 < 2 * SSD_HEADS, -jnp.exp(alog_ref[...]), 0.0)
    row = lax.broadcasted_iota(jnp.int32, (q, q), 0)
    col = lax.broadcasted_iota(jnp.int32, (q, q), 1)
    masks = (col <= row, col >= row)
    lo = _lane_iota(q) < SSD_HEAD_DIM
    lo1 = lane1 < SSD_HEAD_DIM
    streams = [(d, g) for d in range(2) for g in range(SSD_GROUPS)]

    dts, a_cums, a_cum_ts, a_tots = [], [], [], []
    for d in range(2):
        dt = dt_refs[d][0]
        dta = dt * avec
        ones = masks[d].astype(BF16)
        hi = dta.astype(BF16)
        r1 = dta - hi.astype(F32)
        mid = r1.astype(BF16)
        lo3 = (r1 - mid.astype(F32)).astype(BF16)
        a_cum = _dot(ones, hi) + _dot(ones, mid) + _dot(ones, lo3)
        dts.append(dt)
        a_cums.append(a_cum)
        a_cum_ts.append(a_cum.T)
        a_tots.append(a_cum[q - 1:q, :] if d == 0 else a_cum[0:1, :])

    def head_col(arr, d, hh):
        c = SSD_HEADS * d + hh
        return arr[:, c:c + 1]

    bgs, cgs, gmats, xdts, acls, atls = {}, {}, {}, {}, {}, {}
    xss = {}
    for d, g in streams:
        b0 = SSD_INNER + g * SSD_STATE
        c0 = SSD_INNER + (SSD_GROUPS + g) * SSD_STATE
        bgs[d, g] = us[d][:, b0:b0 + SSD_STATE]
        cgs[d, g] = us[d][:, c0:c0 + SSD_STATE]
        xss[d, g] = us[d][:, g * LANES:(g + 1) * LANES].astype(F32)
        gmats[d, g] = _dot_nt(cgs[d, g], bgs[d, g])
    for d, g in streams:
        h0, h1 = 2 * g, 2 * g + 1
        xs = xss[d, g]
        acls[d, g] = jnp.where(lo, head_col(a_cums[d], d, h0), head_col(a_cums[d], d, h1))
        atls[d, g] = jnp.where(lo1, head_col(a_tots[d], d, h0), head_col(a_tots[d], d, h1))
        xdts[d, g] = xs * jnp.where(lo, head_col(dts[d], d, h0), head_col(dts[d], d, h1))

    intra = {}
    for d, g in streams:
        xdt_b = xdts[d, g].astype(BF16)
        ys = []
        for hh in (2 * g, 2 * g + 1):
            c = SSD_HEADS * d + hh
            seg = a_cums[d][:, c:c + 1] - a_cum_ts[d][c:c + 1, :]
            dec = jnp.exp(jnp.where(masks[d], seg, NEG_BIG))
            ys.append(_dot((gmats[d, g] * dec).astype(BF16), xdt_b))
        intra[d, g] = jnp.where(lo, ys[0], ys[1])

    for d, g in streams:
        st = st_ref[d, g]
        y = intra[d, g] + _dot(cgs[d, g], st.astype(BF16)) * jnp.exp(acls[d, g])
        if d == 0:
            y = y + pv_ref[0:1, g * LANES:(g + 1) * LANES] * xss[d, g]
        y_refs[d][0, :, g * LANES:(g + 1) * LANES] = y.astype(y_refs[d].dtype)
        w = (xdts[d, g] * jnp.exp(atls[d, g] - acls[d, g])).astype(BF16)
        st_ref[d, g] = st * jnp.exp(atls[d, g]) + _dot_tn(bgs[d, g], w)


def _ssd(u, dt, alog, pvec, layer, n_lat):
    bsz, t, c = u.shape
    nc = t // SSD_Q
    ncl = n_lat // SSD_Q
    ncc = nc - ncl

    def fwd(s):
        return jnp.where(s < ncc, ncl + s, s - ncc)

    def bwd(s):
        return jnp.where(s < ncc, nc - 1 - s, ncl - 1 - (s - ncc))

    layer_blk = lambda a: pl.BlockSpec((None,) + a.shape[1:], lambda b, s: (layer,) + (0,) * (a.ndim - 1))
    y_shape = jax.ShapeDtypeStruct((bsz, t, SSD_INNER), BF16)
    return pl.pallas_call(
        _ssd_kernel,
        grid=(bsz, nc),
        in_specs=[pl.BlockSpec((1, SSD_Q, c), lambda b, s: (b, fwd(s), 0)),
                  pl.BlockSpec((1, SSD_Q, c), lambda b, s: (b, bwd(s), 0)),
                  pl.BlockSpec((1, SSD_Q, LANES), lambda b, s: (b, fwd(s), 0)),
                  pl.BlockSpec((1, SSD_Q, LANES), lambda b, s: (b, bwd(s), 0)),
                  layer_blk(alog), layer_blk(pvec)],
        out_specs=[pl.BlockSpec((1, SSD_Q, SSD_INNER), lambda b, s: (b, fwd(s), 0)),
                   pl.BlockSpec((1, SSD_Q, SSD_INNER), lambda b, s: (b, bwd(s), 0))],
        out_shape=[y_shape, y_shape],
        scratch_shapes=[pltpu.VMEM((2, SSD_GROUPS, SSD_STATE, LANES), F32)],
        compiler_params=_cparams(("parallel", "arbitrary")),
        name="ssd_scan",
    )(u, u, dt, dt, alog, pvec)


def _outproj_kernel(*refs, final, lat_tiles, with_ctx, split_x):
    refs = list(refs)
    oab_ref = refs.pop(0)
    octx_ref = refs.pop(0) if with_ctx else None
    oc_ref, yf_ref, yb_ref, g_ref, x_ref = refs[:5]
    del refs[:5]
    xctx_ref = refs.pop(0) if split_x else None
    mod_ref, pv_ref, w_ref, fw_ref, o_ref = refs
    is_ctx = pl.program_id(1) >= lat_tiles

    g = g_ref[0].astype(F32)
    oab = (jnp.where(is_ctx, octx_ref[0], oab_ref[0]) if with_ctx else oab_ref[0]).astype(F32)
    x = jnp.where(is_ctx, xctx_ref[0], x_ref[0]) if split_x else x_ref[0]
    m_a = oab[:, 0:256] * g[:, 0:256]
    m_b = oab[:, 256:512] * g[:, 256:512]
    m_c = oc_ref[0].astype(F32) * g[:, 512:768]
    yd = yf_ref[0].astype(F32) + yb_ref[0].astype(F32)
    td = yd * g[:, 768:1024]
    m_d = td * _rms(td, SSD_INNER) * pv_ref[1:2, :]
    acc = _dot(m_a.astype(BF16), w_ref[0:256, :])
    acc += _dot(m_b.astype(BF16), w_ref[256:512, :])
    acc += _dot(m_c.astype(BF16), w_ref[512:768, :])
    acc += _dot(m_d.astype(BF16), w_ref[768:1024, :])
    xn = x + mod_ref[0, 2:3, :] * acc
    if final:
        xn = xn * _rms(xn, xn.shape[-1]) * fw_ref[...]
    o_ref[0] = xn


def _lat_ctx_specs(width, nlt):
    return [pl.BlockSpec((1, TM, width), lambda b, i: (b, jnp.minimum(i, nlt - 1), 0)),
            pl.BlockSpec((1, TM, width), lambda b, i: (b, jnp.maximum(i - nlt, 0), 0))]


def _outproj(oab, oab_ctx, oc, yf, yb, gates, xsrc, mods, pvec, wout, fw, layer, n_lat, final):
    bsz, t, _ = gates.shape
    d = xsrc[0].shape[-1]
    nlt = n_lat // TM
    rows = n_lat if final else t
    with_ctx = oab_ctx is not None
    split_x = len(xsrc) == 2
    tok = lambda w: pl.BlockSpec((1, TM, w), lambda b, i: (b, i, 0))
    layer_blk = lambda a: pl.BlockSpec((None,) + a.shape[1:], lambda b, i: (layer,) + (0,) * (a.ndim - 1))
    oab_specs = _lat_ctx_specs(4 * LANES, nlt) if with_ctx else [tok(4 * LANES)]
    x_specs = _lat_ctx_specs(d, nlt) if split_x else [tok(d)]
    return pl.pallas_call(
        functools.partial(_outproj_kernel, final=final, lat_tiles=nlt, with_ctx=with_ctx, split_x=split_x),
        grid=(bsz, rows // TM),
        in_specs=oab_specs + [tok(2 * LANES), tok(SSD_INNER), tok(SSD_INNER), tok(1024)] + x_specs
        + [pl.BlockSpec((None, 1, 3, d), lambda b, i: (layer, jnp.where(i < nlt, b, bsz), 0, 0)),
           layer_blk(pvec), layer_blk(wout), pl.BlockSpec(fw.shape, lambda b, i: (0, 0))],
        out_specs=tok(d),
        out_shape=jax.ShapeDtypeStruct((bsz, rows, d), F32),
        compiler_params=_cparams(("parallel", "parallel")),
        name="outproj",
    )(oab, *([oab_ctx] if with_ctx else []), oc, yf, yb, gates, *xsrc, mods, pvec, wout, fw)


def _rope_tables(n_ctx, n_lat):
    rows = n_lat // GRID_W
    row = jnp.repeat(jnp.arange(rows, dtype=F32), GRID_W)
    col = jnp.tile(jnp.arange(GRID_W, dtype=F32), rows)

    def cos_sin(rot_dim):
        n_freq = rot_dim // 4
        inv_freq = ROPE_THETA ** (-jnp.arange(n_freq, dtype=F32) / n_freq)
        ang = jnp.concatenate([row[:, None] * inv_freq, col[:, None] * inv_freq], axis=-1)
        return jnp.cos(ang), jnp.sin(ang)

    def with_ctx(tab, fill):
        return jnp.concatenate([tab, jnp.full((n_ctx, LANES), fill, F32)], axis=0)

    lane = np.arange(LANES)
    cos_a, sin_a = cos_sin(MLA_ROPE)
    half = MLA_ROPE // 2
    idx = np.where((lane >= MLA_NOPE) & (lane < MLA_NOPE + MLA_ROPE), (lane - MLA_NOPE) % half, 0)
    first = (lane >= MLA_NOPE) & (lane < MLA_NOPE + half)
    second = (lane >= MLA_NOPE + half) & (lane < MLA_NOPE + MLA_ROPE)
    rot = first | second
    ca = jnp.where(rot, cos_a[:, idx], 1.0)
    sa_up = jnp.where(first, -sin_a[:, idx], 0.0)
    sa_dn = jnp.where(second, sin_a[:, idx], 0.0)
    cos_h, sin_h = cos_sin(HEAD_DIM)
    half = HEAD_DIM // 2
    idx = lane % half
    first = (lane % HEAD_DIM) < half
    cb = cos_h[:, idx]
    sb_up = jnp.where(first, -sin_h[:, idx], 0.0)
    sb_dn = jnp.where(first, 0.0, sin_h[:, idx])
    return jnp.stack([with_ctx(ca, 1.0), with_ctx(sa_up, 0.0), with_ctx(sa_dn, 0.0),
                      with_ctx(cb, 1.0), with_ctx(sb_up, 0.0), with_ctx(sb_dn, 0.0)])


def _relayout_w_in(w_in):
    depth, d, _ = w_in.shape
    w = w_in.astype(BF16)
    kr0 = MLA_Q_RANK + MLA_KV_RANK
    dt0 = w_in.shape[-1] - 2 * SSD_HEADS
    z = lambda n: jnp.zeros((depth, d, n), BF16)
    cat = jnp.concatenate([w[..., :kr0], w[..., dt0:], z(MLA_NOPE - 2 * SSD_HEADS), w[..., kr0:kr0 + MLA_ROPE],
                           z(LANES - MLA_NOPE - MLA_ROPE), w[..., kr0 + MLA_ROPE:dt0]], axis=-1)
    assert cat.shape[-1] == C_END
    return cat


def _relayout_mla(w_q_up, w_kv_up):
    depth = w_q_up.shape[0]
    wq = w_q_up.reshape(depth, MLA_Q_RANK, MLA_HEADS, MLA_NOPE + MLA_ROPE)
    wq = jnp.pad(wq, ((0, 0), (0, 0), (0, 0), (0, LANES - MLA_NOPE - MLA_ROPE)))
    wq = wq.reshape(depth, MLA_Q_RANK, MLA_HEADS * LANES)
    wkv = w_kv_up.reshape(depth, MLA_KV_RANK, MLA_HEADS, MLA_NOPE + MLA_V)
    wk = jnp.pad(wkv[..., :MLA_NOPE], ((0, 0), (0, 0), (0, 0), (0, LANES - MLA_NOPE)))
    wk = wk.reshape(depth, MLA_KV_RANK, MLA_HEADS * LANES)
    wv = wkv[..., MLA_NOPE:].reshape(depth, MLA_KV_RANK, MLA_HEADS * MLA_V)
    return wq.astype(BF16), jnp.concatenate([wk, wv], axis=-1).astype(BF16)


def _pad_lanes(v, width):
    return jnp.pad(v, ((0, 0), (0, width - v.shape[-1])))


def kernel(x, c, ctx, c_ctx, norm_w, w_mod, b_mod, w_in, mla_q_norm, mla_kv_norm, mla_w_q_up, mla_w_kv_up,
           gqa_q_norm, gqa_k_norm, swa_sink, ssd_conv_w, ssd_conv_b, ssd_a_log, ssd_dt_bias, ssd_d,
           ssd_norm_w, w_out, final_norm_w):
    bsz, n_lat, d = x.shape
    n_ctx = ctx.shape[1]
    depth = w_in.shape[0]
    assert d == 1024 and bsz % NB_IN == 0 and bsz + NB_IN <= 8
    assert n_ctx % TM == 0 and n_ctx % TQ_CTX == 0 and n_ctx % TQ_SWA == 0 and n_ctx % SSD_Q == 0
    assert n_lat % TQ == 0 and n_lat % n_ctx == 0 and n_lat % GRID_W == 0
    assert n_lat >= TQ_SWA + 2 * WINDOW

    cvec = jnp.concatenate([c, jnp.broadcast_to(c_ctx, (NB_IN, d)), jnp.zeros((8 - bsz - NB_IN, d), F32)], axis=0)
    mods = _modulation(cvec, w_mod, b_mod).reshape(depth, 8, 3, d)[:, :bsz + NB_IN]

    wcat = _relayout_w_in(w_in)
    wq, wkv = _relayout_mla(mla_w_q_up, mla_w_kv_up)
    wout = w_out.astype(BF16)
    ropes = _rope_tables(n_ctx, n_lat)
    pv_in = jnp.stack([mla_q_norm,
                       _pad_lanes(mla_kv_norm, 256),
                       jnp.tile(gqa_q_norm, (1, 4)),
                       _pad_lanes(jnp.tile(gqa_k_norm, (1, 2)), 256),
                       _pad_lanes(ssd_dt_bias.reshape(depth, -1), 256)], axis=1)
    pv_in = jnp.pad(pv_in, ((0, 0), (0, 3), (0, 0)))
    pv_out = jnp.stack([jnp.repeat(ssd_d, SSD_HEAD_DIM, axis=-1), ssd_norm_w], axis=1)
    pv_out = jnp.pad(pv_out, ((0, 0), (0, 6), (0, 0)))
    sink_rows = jnp.broadcast_to(jnp.pad(swa_sink, ((0, 0), (0, 4)))[:, :, None], (depth, 8, LANES))
    conv_w8 = jnp.pad(ssd_conv_w, ((0, 0), (0, 8 - SSD_CONV), (0, 0)))
    alog = _pad_lanes(ssd_a_log.reshape(depth, -1), LANES).reshape(depth, 1, LANES)
    conv_b = ssd_conv_b.reshape(depth, 1, -1)
    nw = norm_w.reshape(depth, 1, d)
    fw = final_norm_w.reshape(1, d)

    xsrc = (x, ctx)
    for l in range(depth):
        final = l == depth - 1
        qall, kall, vall, swa_in, gates, xbc, dt = _inproj(xsrc, mods, nw, wcat, wq, wkv, pv_in, ropes, l, n_lat)
        oab = _attention_latent(qall, kall, vall, n_lat)
        oab_ctx = None if final else _attention_ctx(qall, kall, vall, n_lat, n_ctx)
        oc = _swa(swa_in, sink_rows, l, n_lat, n_ctx)
        u = _conv(xbc, conv_w8, conv_b, l, n_lat)
        yf, yb = _ssd(u, dt, alog, pv_out, l, n_lat)
        xsrc = (_outproj(oab, oab_ctx, oc, yf, yb, gates, xsrc, mods, pv_out, wout, fw, l, n_lat, final),)
    return xsrc[0]
```

```python
import functools

import numpy as np
import jax
import jax.numpy as jnp
from jax import lax
from jax.experimental import pallas as pl
from jax.experimental.pallas import tpu as pltpu

F32 = jnp.float32
BF16 = jnp.bfloat16

EPS = 1e-6
GRID_W = 64
ROPE_THETA = 10000.0
HEAD_DIM = 64
MLA_HEADS, MLA_NOPE, MLA_ROPE, MLA_V = 4, 64, 32, 64
MLA_Q_RANK, MLA_KV_RANK = 256, 128
GQA_HEADS, GQA_KV_HEADS = 4, 2
SWA_HEADS, SWA_KV_HEADS = 4, 2
WINDOW = 128
SSD_HEADS, SSD_HEAD_DIM, SSD_GROUPS, SSD_STATE, SSD_CONV = 4, 64, 2, 128, 5
SSD_INNER = SSD_HEADS * SSD_HEAD_DIM
SSD_CONV_DIM = SSD_INNER + 2 * SSD_GROUPS * SSD_STATE

LANES = 128
LOG2E = 1.4426950408889634
NEG_BIG = -1e30

C_CQ, C_CKV, C_KR, C_GA = 0, 256, 384, 512
C_QB, C_KB, C_VB, C_GB = 768, 1024, 1152, 1280
C_QC, C_KC, C_VC, C_GC = 1536, 1792, 1920, 2048
C_Z, C_XBC, C_END = 2304, 2560, 3328
C_DT = C_KR

TM = 256
NB_IN = 2
TQ = 1024
TQ_CTX = 256
KV_CHUNK = 1024
TQ_SWA = 256
SSD_Q = 256
HALO = 8
VMEM_LIMIT = 56 * 1024 * 1024


def _cparams(sem):
    return pltpu.CompilerParams(dimension_semantics=sem, vmem_limit_bytes=VMEM_LIMIT)


def _sigmoid(x):
    return 1.0 / (1.0 + jnp.exp(-x))


def _silu(x):
    return x * _sigmoid(x)


def _rms(x, n):
    return lax.rsqrt(jnp.sum(x * x, axis=-1, keepdims=True) * (1.0 / n) + EPS)


def _dot(a, b):
    return jnp.dot(a, b, preferred_element_type=F32)


def _dot_nt(a, b):
    return lax.dot_general(a, b, (((1,), (1,)), ((), ())), preferred_element_type=F32)


def _dot_tn(a, b):
    return lax.dot_general(a, b, (((0,), (0,)), ((), ())), preferred_element_type=F32)


def _lane_iota(rows):
    return lax.broadcasted_iota(jnp.int32, (rows, LANES), 1)


def _mod_kernel(c_ref, w_ref, b_ref, o_ref):
    cs = _silu(c_ref[...])
    o_ref[0] = _dot(cs.astype(BF16), w_ref[0].astype(BF16)) + b_ref[0]


def _modulation(cvec, w_mod, b_mod):
    depth, d, d3 = w_mod.shape
    tn = 512
    return pl.pallas_call(
        _mod_kernel,
        grid=(depth, d3 // tn),
        in_specs=[pl.BlockSpec((8, d), lambda l, j: (0, 0)),
                  pl.BlockSpec((1, d, tn), lambda l, j: (l, 0, j)),
                  pl.BlockSpec((1, 1, tn), lambda l, j: (l, 0, j))],
        out_specs=pl.BlockSpec((1, 8, tn), lambda l, j: (l, 0, j)),
        out_shape=jax.ShapeDtypeStruct((depth, 8, d3), F32),
        compiler_params=_cparams(("parallel", "parallel")),
        name="modulation",
    )(cvec, w_mod, b_mod.reshape(depth, 1, d3))


def _rope(x, c, s_up, s_dn, half):
    return x * c + pltpu.roll(x, LANES - half, 1) * s_up + pltpu.roll(x, half, 1) * s_dn


def _head_rms(blk, lane):
    sq = blk * blk
    lo = lane < HEAD_DIM
    s_lo = jnp.sum(jnp.where(lo, sq, 0.0), axis=-1, keepdims=True)
    s_hi = jnp.sum(jnp.where(lo, 0.0, sq), axis=-1, keepdims=True)
    r_lo = lax.rsqrt(s_lo * (1.0 / HEAD_DIM) + EPS)
    r_hi = lax.rsqrt(s_hi * (1.0 / HEAD_DIM) + EPS)
    return jnp.where(lo, r_lo, r_hi)


def _dup_halves(blk, lane):
    sw = pltpu.roll(blk, HEAD_DIM, 1)
    lo = lane < HEAD_DIM
    return jnp.where(lo, blk, sw), jnp.where(lo, sw, blk)


def _store_v_slots(v_ref, bb, slot, pair, lo):
    v_ref[bb, :, slot * LANES:(slot + 1) * LANES] = jnp.where(lo, pair, 1.0).astype(BF16)
    v_ref[bb, :, (slot + 1) * LANES:(slot + 2) * LANES] = jnp.where(lo, 1.0, pair).astype(BF16)


def _inproj_kernel(*refs, lat_tiles, split_x):
    refs = list(refs)
    x_ref = refs.pop(0)
    xctx_ref = refs.pop(0) if split_x else None
    (mod_ref, nw_ref, wcat_ref, wq_ref, wkv_ref, pv_ref, rope_ref,
     qall_ref, kall_ref, vall_ref, swa_ref, g_ref, xbc_ref, dt_ref) = refs
    nb, tm, d = x_ref.shape
    lane = _lane_iota(tm)
    lo = lane < HEAD_DIM
    is_ctx = pl.program_id(1) >= lat_tiles

    hs = []
    for bb in range(nb):
        x = jnp.where(is_ctx, xctx_ref[bb], x_ref[bb]) if split_x else x_ref[bb]
        xn = x * _rms(x, d)
        h = (xn * nw_ref[...]) * (1.0 + mod_ref[bb, 1:2, :]) + mod_ref[bb, 0:1, :]
        hs.append(h.astype(BF16))
    hcat = jnp.concatenate(hs, axis=0)

    bounds = (C_CQ, C_QB, C_QC, C_Z, C_END)
    proj = {}

    def project(gi):
        proj[gi] = _dot(hcat, wcat_ref[:, bounds[gi]:bounds[gi + 1]])

    def cols(bb, c0, width):
        gi = max(i for i in range(4) if bounds[i] <= c0)
        off = c0 - bounds[gi]
        rows = slice(None) if bb is None else slice(bb * tm, (bb + 1) * tm)
        return proj[gi][rows, off:off + width]

    ca, sa_up, sa_dn = rope_ref[0], rope_ref[1], rope_ref[2]
    cb, sb_up, sb_dn = rope_ref[3], rope_ref[4], rope_ref[5]
    qscale_a = (MLA_NOPE + MLA_ROPE) ** -0.5 * LOG2E
    qscale_h = HEAD_DIM ** -0.5 * LOG2E
    base = MLA_HEADS * LANES

    def stage_a():
        cq = cols(None, C_CQ, MLA_Q_RANK)
        cqn = cq * _rms(cq, MLA_Q_RANK) * pv_ref[0:1, :]
        qa_all = _dot(cqn.astype(BF16), wq_ref[...])
        ckv = cols(None, C_CKV, MLA_KV_RANK)
        ckvn = ckv * _rms(ckv, MLA_KV_RANK) * pv_ref[1:2, 0:MLA_KV_RANK]
        kv_all = _dot(ckvn.astype(BF16), wkv_ref[...])
        for bb in range(nb):
            rows = slice(bb * tm, (bb + 1) * tm)
            qa, kv = qa_all[rows], kv_all[rows]
            kr = jnp.where(lane >= MLA_NOPE, _rope(cols(bb, C_KR, LANES), ca, sa_up, sa_dn, MLA_ROPE // 2), 0.0)
            for hh in range(MLA_HEADS):
                sl = slice(hh * LANES, (hh + 1) * LANES)
                qall_ref[bb, :, sl] = (_rope(qa[:, sl], ca, sa_up, sa_dn, MLA_ROPE // 2) * qscale_a).astype(BF16)
                kall_ref[bb, :, sl] = (kv[:, sl] + kr).astype(BF16)
            for pr in range(2):
                _store_v_slots(vall_ref, bb, 2 * pr, kv[:, (4 + pr) * LANES:(5 + pr) * LANES], lo)
            g_ref[bb, :, 0:256] = _silu(cols(bb, C_GA, 256)).astype(BF16)

    def stage_b(bb):
        for blk_i in range(2):
            qb = cols(bb, C_QB + blk_i * LANES, LANES)
            qb = qb * _head_rms(qb, lane) * pv_ref[2:3, 0:LANES]
            qb = _rope(qb, cb, sb_up, sb_dn, HEAD_DIM // 2) * qscale_h
            c0 = base + 2 * blk_i * LANES
            qall_ref[bb, :, c0:c0 + LANES] = jnp.where(lo, qb, 0.0).astype(BF16)
            qall_ref[bb, :, c0 + LANES:c0 + 2 * LANES] = jnp.where(lo, 0.0, qb).astype(BF16)
        kb = cols(bb, C_KB, LANES)
        kb = kb * _head_rms(kb, lane) * pv_ref[3:4, 0:LANES]
        kb = _rope(kb, cb, sb_up, sb_dn, HEAD_DIM // 2)
        k0, k1 = _dup_halves(kb, lane)
        k0 = k0.astype(BF16)
        k1 = k1.astype(BF16)
        kall_ref[bb, :, base:base + LANES] = k0
        kall_ref[bb, :, base + LANES:base + 2 * LANES] = k0
        kall_ref[bb, :, base + 2 * LANES:base + 3 * LANES] = k1
        kall_ref[bb, :, base + 3 * LANES:base + 4 * LANES] = k1
        v0, v1 = _dup_halves(cols(bb, C_VB, LANES), lane)
        _store_v_slots(vall_ref, bb, 4, v0, lo)
        _store_v_slots(vall_ref, bb, 6, v1, lo)
        g_ref[bb, :, 256:512] = _silu(cols(bb, C_GB, 256)).astype(BF16)

    def stage_c(bb):
        for blk_i in range(2):
            qc = cols(bb, C_QC + blk_i * LANES, LANES)
            swa_ref[bb, :, blk_i * LANES:(blk_i + 1) * LANES] = (
                _rope(qc, cb, sb_up, sb_dn, HEAD_DIM // 2) * qscale_h).astype(BF16)
        kc = _rope(cols(bb, C_KC, LANES), cb, sb_up, sb_dn, HEAD_DIM // 2)
        kc0, kc1 = _dup_halves(kc, lane)
        swa_ref[bb, :, 2 * LANES:3 * LANES] = kc0.astype(BF16)
        swa_ref[bb, :, 3 * LANES:4 * LANES] = kc1.astype(BF16)
        vc0, vc1 = _dup_halves(cols(bb, C_VC, LANES), lane)
        _store_v_slots(swa_ref, bb, 4, vc0, lo)
        _store_v_slots(swa_ref, bb, 6, vc1, lo)
        g_ref[bb, :, 512:768] = _silu(cols(bb, C_GC, 256)).astype(BF16)

    def stage_d(bb):
        g_ref[bb, :, 768:1024] = _silu(cols(bb, C_Z, 256)).astype(BF16)
        xbc_ref[bb] = cols(bb, C_XBC, SSD_CONV_DIM)
        dtr = cols(bb, C_DT, LANES) + pv_ref[4:5, 0:LANES]
        dt_ref[bb] = jnp.maximum(dtr, 0.0) + jnp.log(1.0 + jnp.exp(-jnp.abs(dtr)))

    project(0)
    project(1)
    stage_a()
    project(2)
    for bb in range(nb):
        stage_b(bb)
    project(3)
    for bb in range(nb):
        stage_c(bb)
    for bb in range(nb):
        stage_d(bb)


def _inproj(xsrc, mods, nw, wcat, wq, wkv, pvec, ropes, layer, n_lat):
    bsz, _, d = xsrc[0].shape
    t = ropes.shape[1]
    nlt = n_lat // TM
    split_x = len(xsrc) == 2
    grid = (bsz // NB_IN, t // TM)
    tok = lambda w: pl.BlockSpec((NB_IN, TM, w), lambda b, i: (b, i, 0))
    layer_blk = lambda a: pl.BlockSpec((None,) + a.shape[1:], lambda b, i: (layer,) + (0,) * (a.ndim - 1))
    x_specs = [tok(d)]
    if split_x:
        x_specs = [pl.BlockSpec((NB_IN, TM, d), lambda b, i: (b, jnp.minimum(i, nlt - 1), 0)),
                   pl.BlockSpec((NB_IN, TM, d), lambda b, i: (b, jnp.maximum(i - nlt, 0), 0))]
    widths = (8 * LANES, 8 * LANES, 8 * LANES, 8 * LANES, 1024, SSD_CONV_DIM, LANES)
    dtypes = (BF16, BF16, BF16, BF16, BF16, F32, F32)
    return pl.pallas_call(
        functools.partial(_inproj_kernel, lat_tiles=nlt, split_x=split_x),
        grid=grid,
        in_specs=x_specs + [
            pl.BlockSpec((None, NB_IN, 3, d), lambda b, i: (layer, jnp.where(i < nlt, b, bsz // NB_IN), 0, 0)),
            layer_blk(nw), layer_blk(wcat), layer_blk(wq), layer_blk(wkv), layer_blk(pvec),
            pl.BlockSpec((6, TM, LANES), lambda b, i: (0, i, 0))],
        out_specs=[tok(w) for w in widths],
        out_shape=[jax.ShapeDtypeStruct((bsz, t, w), dt) for w, dt in zip(widths, dtypes)],
        compiler_params=_cparams(("parallel", "parallel")),
        name="inproj",
    )(*xsrc, mods, nw, wcat, wq, wkv, pvec, ropes)


def _kv_chunks(nk):
    return [(c0, min(KV_CHUNK, nk - c0)) for c0 in range(0, nk, KV_CHUNK)]


def _logits_pass(q, k_ref, s_ref, sl, nk):
    mrun = None
    for c0, cw in _kv_chunks(nk):
        s = _dot_nt(q, k_ref[0, c0:c0 + cw, sl])
        s_ref[:, c0:c0 + cw] = s
        for t0 in range(0, cw, LANES):
            piece = s[:, t0:t0 + LANES]
            mrun = piece if mrun is None else jnp.maximum(mrun, piece)
    return jnp.max(mrun, axis=-1, keepdims=True)


def _pv_pass(m, v_ref, s_ref, sl, nk):
    acc = None
    for c0, cw in _kv_chunks(nk):
        pr = jnp.exp2(s_ref[:, c0:c0 + cw] - m).astype(BF16)
        o = _dot(pr, v_ref[0, c0:c0 + cw, sl])
        acc = o if acc is None else acc + o
    return acc / pltpu.roll(acc, HEAD_DIM, 1)


def _attn_body(q_ref, k_ref, v_ref, g_ref, o_ref, s0_ref, s1_ref):
    tq = q_ref.shape[1]
    nk = k_ref.shape[1]
    lane = _lane_iota(tq)
    sls = (slice(0, LANES), slice(LANES, 2 * LANES))
    s_refs = (s0_ref, s1_ref)
    ms = [_logits_pass(q_ref[0, :, sls[j]], k_ref, s_refs[j], sls[j], nk) for j in range(2)]
    outs = [_pv_pass(ms[j], v_ref, s_refs[j], sls[j], nk) for j in range(2)]
    o_ref[0] = (jnp.where(lane < HEAD_DIM, outs[0], outs[1]) * g_ref[0].astype(F32)).astype(o_ref.dtype)


def _attention_latent(qall, kall, vall, gates, n_lat):
    bsz, t, _ = qall.shape
    return pl.pallas_call(
        _attn_body,
        grid=(bsz, 4, n_lat // TQ),
        in_specs=[pl.BlockSpec((1, TQ, 2 * LANES), lambda b, p, i: (b, i, p)),
                  pl.BlockSpec((1, t, 2 * LANES), lambda b, p, i: (b, 0, p)),
                  pl.BlockSpec((1, t, 2 * LANES), lambda b, p, i: (b, 0, p)),
                  pl.BlockSpec((1, TQ, LANES), lambda b, p, i: (b, i, p))],
        out_specs=pl.BlockSpec((1, TQ, LANES), lambda b, p, i: (b, i, p)),
        out_shape=jax.ShapeDtypeStruct((bsz, n_lat, 4 * LANES), BF16),
        scratch_shapes=[pltpu.VMEM((TQ, t), F32), pltpu.VMEM((TQ, t), F32)],
        compiler_params=_cparams(("parallel", "parallel", "arbitrary")),
        name="attn_global",
    )(qall, kall, vall, gates)


def _attention_ctx(qall, kall, vall, gates, n_lat, n_ctx):
    bsz, t, _ = qall.shape
    qoff = n_lat // TQ_CTX
    koff = n_lat // n_ctx
    return pl.pallas_call(
        _attn_body,
        grid=(bsz, 4, n_ctx // TQ_CTX),
        in_specs=[pl.BlockSpec((1, TQ_CTX, 2 * LANES), lambda b, p, i: (b, qoff + i, p)),
                  pl.BlockSpec((1, n_ctx, 2 * LANES), lambda b, p, i: (b, koff, p)),
                  pl.BlockSpec((1, n_ctx, 2 * LANES), lambda b, p, i: (b, koff, p)),
                  pl.BlockSpec((1, TQ_CTX, LANES), lambda b, p, i: (b, qoff + i, p))],
        out_specs=pl.BlockSpec((1, TQ_CTX, LANES), lambda b, p, i: (b, i, p)),
        out_shape=jax.ShapeDtypeStruct((bsz, n_ctx, 4 * LANES), BF16),
        scratch_shapes=[pltpu.VMEM((TQ_CTX, n_ctx), F32), pltpu.VMEM((TQ_CTX, n_ctx), F32)],
        compiler_params=_cparams(("parallel", "parallel", "arbitrary")),
        name="attn_global_ctx",
    )(qall, kall, vall, gates)


def _swa_kernel(q_ref, k_ref, v_ref, bias_ref, sink_ref, g_ref, o_ref, *, n_lat, n_ctx):
    tq = q_ref.shape[1]
    band = tq + 2 * WINDOW
    i = pl.program_id(1)
    lane = _lane_iota(tq)
    lo = lane < HEAD_DIM

    start = pl.multiple_of(jnp.clip(i * tq - WINDOW, 0, n_lat - band), WINDOW)
    bias = bias_ref[0]
    k = jnp.concatenate([k_ref[0, n_lat:n_lat + n_ctx, :], k_ref[0, pl.ds(start, band), :]], axis=0)
    v = jnp.concatenate([v_ref[0, n_lat:n_lat + n_ctx, :], v_ref[0, pl.ds(start, band), :]], axis=0)

    def logits_of(hh):
        ps = slice((hh // 2) * LANES, (hh // 2 + 1) * LANES)
        q = q_ref[0, :, ps]
        zero = jnp.zeros_like(q)
        qj = jnp.where(lo, q, zero) if hh % 2 == 0 else jnp.where(lo, zero, q)
        s = _dot_nt(qj, k[:, ps]) + bias
        sink = sink_ref[hh:hh + 1, 0:1] * LOG2E
        return s, sink, jnp.maximum(jnp.max(s, axis=-1, keepdims=True), sink)

    ahead = 2
    logits = [logits_of(hh) for hh in range(ahead)]
    outs = []
    for hh in range(SWA_HEADS):
        if hh + ahead < SWA_HEADS:
            logits.append(logits_of(hh + ahead))
        s, sink, m = logits[hh]
        acc = _dot(jnp.exp2(s - m).astype(BF16), v[:, hh * LANES:(hh + 1) * LANES])
        outs.append(acc / (pltpu.roll(acc, HEAD_DIM, 1) + jnp.exp2(sink - m)))
    for pair in range(2):
        ps = slice(pair * LANES, (pair + 1) * LANES)
        o_ref[0, :, ps] = (jnp.where(lo, outs[2 * pair], outs[2 * pair + 1])
                           * g_ref[0, :, ps].astype(F32)).astype(o_ref.dtype)


def _swa_bias(n_ctx):
    band = TQ_SWA + 2 * WINDOW
    r = np.arange(TQ_SWA)[:, None]
    c = np.arange(band)[None, :]
    ok = [np.abs(c - shift - r) <= WINDOW for shift in (0, WINDOW, 2 * WINDOW)] + [np.zeros((TQ_SWA, band), bool)]
    lat = np.where(np.stack(ok), 0.0, NEG_BIG).astype(np.float32)
    return jnp.asarray(np.concatenate([np.zeros((4, TQ_SWA, n_ctx), np.float32), lat], axis=-1))


def _swa(swa_in, sink_rows, gates, layer, n_lat, n_ctx):
    bsz, t, _ = swa_in.shape
    nlt = n_lat // TQ_SWA
    bias = _swa_bias(n_ctx)

    def kind(i):
        return jnp.where(i >= nlt, 3, jnp.where(i == 0, 0, jnp.where(i == nlt - 1, 2, 1)))

    return pl.pallas_call(
        functools.partial(_swa_kernel, n_lat=n_lat, n_ctx=n_ctx),
        grid=(bsz, t // TQ_SWA),
        in_specs=[pl.BlockSpec((1, TQ_SWA, 2 * LANES), lambda b, i: (b, i, 0)),
                  pl.BlockSpec((1, t, 2 * LANES), lambda b, i: (b, 0, 1)),
                  pl.BlockSpec((1, t, 4 * LANES), lambda b, i: (b, 0, 1)),
                  pl.BlockSpec((1,) + bias.shape[1:], lambda b, i: (kind(i), 0, 0)),
                  pl.BlockSpec((None,) + sink_rows.shape[1:], lambda b, i: (layer, 0, 0)),
                  pl.BlockSpec((1, TQ_SWA, 2 * LANES), lambda b, i: (b, i, 2))],
        out_specs=pl.BlockSpec((1, TQ_SWA, 2 * LANES), lambda b, i: (b, i, 0)),
        out_shape=jax.ShapeDtypeStruct((bsz, t, 2 * LANES), BF16),
        compiler_params=_cparams(("parallel", "arbitrary")),
        name="attn_window",
    )(swa_in, swa_in, swa_in, bias, sink_rows, gates)


def _conv_kernel(x_ref, prev_ref, next_ref, w_ref, b_ref, u_ref, *, lat_tiles):
    tm = x_ref.shape[1]
    i = pl.program_id(1)
    first = (i == 0) | (i == lat_tiles)
    last = (i == lat_tiles - 1) | (i == pl.num_programs(1) - 1)
    ext = jnp.concatenate([jnp.where(first, 0.0, prev_ref[0]), x_ref[0], jnp.where(last, 0.0, next_ref[0])], axis=0)
    n_ext = tm + 2 * HALO
    acc = x_ref[0] * w_ref[SSD_CONV // 2:SSD_CONV // 2 + 1, :] + b_ref[...]
    for tap in range(SSD_CONV):
        if tap != SSD_CONV // 2:
            shifted = pltpu.roll(ext, (SSD_CONV // 2 - tap) % n_ext, 0)
            acc = acc + shifted[HALO:HALO + tm, :] * w_ref[tap:tap + 1, :]
    u_ref[0] = _silu(acc).astype(u_ref.dtype)


def _conv(xbc, conv_w8, conv_b, layer, n_lat):
    bsz, t, c = xbc.shape
    per = TM // HALO
    nh = t // HALO
    return pl.pallas_call(
        functools.partial(_conv_kernel, lat_tiles=n_lat // TM),
        grid=(bsz, t // TM),
        in_specs=[pl.BlockSpec((1, TM, c), lambda b, i: (b, i, 0)),
                  pl.BlockSpec((1, HALO, c), lambda b, i: (b, jnp.maximum(i * per - 1, 0), 0)),
                  pl.BlockSpec((1, HALO, c), lambda b, i: (b, jnp.minimum((i + 1) * per, nh - 1), 0)),
                  pl.BlockSpec((None,) + conv_w8.shape[1:], lambda b, i: (layer, 0, 0)),
                  pl.BlockSpec((None,) + conv_b.shape[1:], lambda b, i: (layer, 0, 0))],
        out_specs=pl.BlockSpec((1, TM, c), lambda b, i: (b, i, 0)),
        out_shape=jax.ShapeDtypeStruct((bsz, t, c), BF16),
        compiler_params=_cparams(("parallel", "parallel")),
        name="ssd_conv",
    )(xbc, xbc, xbc, conv_w8, conv_b)


def _ssd_kernel(uf_ref, ub_ref, dtf_ref, dtb_ref, alog_ref, pv_ref, yf_ref, yb_ref, st_ref):
    q = uf_ref.shape[1]
    dt_refs, y_refs = (dtf_ref, dtb_ref), (yf_ref, yb_ref)
    us = (uf_ref[0], ub_ref[0])

    @pl.when(pl.program_id(1) == 0)
    def _():
        st_ref[...] = jnp.zeros_like(st_ref)

    lane1 = lax.broadcasted_iota(jnp.int32, (1, LANES), 1)
    avec = jnp.where(lane1 < 2 * SSD_HEADS, -jnp.exp(alog_ref[...]), 0.0)
    row = lax.broadcasted_iota(jnp.int32, (q, q), 0)
    col = lax.broadcasted_iota(jnp.int32, (q, q), 1)
    masks = (col <= row, col >= row)
    lo = _lane_iota(q) < SSD_HEAD_DIM
    lo1 = lane1 < SSD_HEAD_DIM
    streams = [(d, g) for d in range(2) for g in range(SSD_GROUPS)]

    dts, a_cums, a_cum_ts, a_tots = [], [], [], []
    for d in range(2):
        dt = dt_refs[d][0]
        dta = dt * avec
        ones = masks[d].astype(BF16)
        hi = dta.astype(BF16)
        r1 = dta - hi.astype(F32)
        mid = r1.astype(BF16)
        lo3 = (r1 - mid.astype(F32)).astype(BF16)
        a_cum = _dot(ones, hi) + _dot(ones, mid) + _dot(ones, lo3)
        dts.append(dt)
        a_cums.append(a_cum)
        a_cum_ts.append(a_cum.T)
        a_tots.append(a_cum[q - 1:q, :] if d == 0 else a_cum[0:1, :])

    def head_col(arr, d, hh):
        c = SSD_HEADS * d + hh
        return arr[:, c:c + 1]

    bgs, cgs, gmats, xdts, acls, atls = {}, {}, {}, {}, {}, {}
    xss = {}
    for d, g in streams:
        b0 = SSD_INNER + g * SSD_STATE
        c0 = SSD_INNER + (SSD_GROUPS + g) * SSD_STATE
        bgs[d, g] = us[d][:, b0:b0 + SSD_STATE]
        cgs[d, g] = us[d][:, c0:c0 + SSD_STATE]
        xss[d, g] = us[d][:, g * LANES:(g + 1) * LANES].astype(F32)
        gmats[d, g] = _dot_nt(cgs[d, g], bgs[d, g])
    for d, g in streams:
        h0, h1 = 2 * g, 2 * g + 1
        xs = xss[d, g]
        acls[d, g] = jnp.where(lo, head_col(a_cums[d], d, h0), head_col(a_cums[d], d, h1))
        atls[d, g] = jnp.where(lo1, head_col(a_tots[d], d, h0), head_col(a_tots[d], d, h1))
        xdts[d, g] = xs * jnp.where(lo, head_col(dts[d], d, h0), head_col(dts[d], d, h1))

    intra = {}
    for d, g in streams:
        xdt_b = xdts[d, g].astype(BF16)
        ys = []
        for hh in (2 * g, 2 * g + 1):
            c = SSD_HEADS * d + hh
            seg = a_cums[d][:, c:c + 1] - a_cum_ts[d][c:c + 1, :]
            dec = jnp.exp(jnp.where(masks[d], seg, NEG_BIG))
            ys.append(_dot((gmats[d, g] * dec).astype(BF16), xdt_b))
        intra[d, g] = jnp.where(lo, ys[0], ys[1])

    for d, g in streams:
        st = st_ref[d, g]
        y = intra[d, g] + _dot(cgs[d, g], st.astype(BF16)) * jnp.exp(acls[d, g])
        if d == 0:
            y = y + pv_ref[0:1, g * LANES:(g + 1) * LANES] * xss[d, g]
        y_refs[d][0, :, g * LANES:(g + 1) * LANES] = y.astype(y_refs[d].dtype)
        w = (xdts[d, g] * jnp.exp(atls[d, g] - acls[d, g])).astype(BF16)
        st_ref[d, g] = st * jnp.exp(atls[d, g]) + _dot_tn(bgs[d, g], w)


def _ssd(u, dt, alog, pvec, layer, n_lat):
    bsz, t, c = u.shape
    nc = t // SSD_Q
    ncl = n_lat // SSD_Q
    ncc = nc - ncl

    def fwd(s):
        return jnp.where(s < ncc, ncl + s, s - ncc)

    def bwd(s):
        return jnp.where(s < ncc, nc - 1 - s, ncl - 1 - (s - ncc))

    layer_blk = lambda a: pl.BlockSpec((None,) + a.shape[1:], lambda b, s: (layer,) + (0,) * (a.ndim - 1))
    y_shape = jax.ShapeDtypeStruct((bsz, t, SSD_INNER), BF16)
    return pl.pallas_call(
        _ssd_kernel,
        grid=(bsz, nc),
        in_specs=[pl.BlockSpec((1, SSD_Q, c), lambda b, s: (b, fwd(s), 0)),
                  pl.BlockSpec((1, SSD_Q, c), lambda b, s: (b, bwd(s), 0)),
                  pl.BlockSpec((1, SSD_Q, LANES), lambda b, s: (b, fwd(s), 0)),
                  pl.BlockSpec((1, SSD_Q, LANES), lambda b, s: (b, bwd(s), 0)),
                  layer_blk(alog), layer_blk(pvec)],
        out_specs=[pl.BlockSpec((1, SSD_Q, SSD_INNER), lambda b, s: (b, fwd(s), 0)),
                   pl.BlockSpec((1, SSD_Q, SSD_INNER), lambda b, s: (b, bwd(s), 0))],
        out_shape=[y_shape, y_shape],
        scratch_shapes=[pltpu.VMEM((2, SSD_GROUPS, SSD_STATE, LANES), F32)],
        compiler_params=_cparams(("parallel", "arbitrary")),
        name="ssd_scan",
    )(u, u, dt, dt, alog, pvec)


def _outproj_kernel(*refs, final, lat_tiles, with_ctx, split_x):
    refs = list(refs)
    oab_ref = refs.pop(0)
    octx_ref = refs.pop(0) if with_ctx else None
    oc_ref, yf_ref, yb_ref, g_ref, x_ref = refs[:5]
    del refs[:5]
    xctx_ref = refs.pop(0) if split_x else None
    mod_ref, pv_ref, w_ref, fw_ref, o_ref = refs
    is_ctx = pl.program_id(1) >= lat_tiles

    oab = jnp.where(is_ctx, octx_ref[0], oab_ref[0]) if with_ctx else oab_ref[0]
    x = jnp.where(is_ctx, xctx_ref[0], x_ref[0]) if split_x else x_ref[0]
    yd = yf_ref[0].astype(F32) + yb_ref[0].astype(F32)
    td = yd * g_ref[0].astype(F32)
    m_d = td * _rms(td, SSD_INNER) * pv_ref[1:2, :]
    acc = _dot(oab, w_ref[0:512, :])
    acc += _dot(oc_ref[0], w_ref[512:768, :])
    acc += _dot(m_d.astype(BF16), w_ref[768:1024, :])
    xn = x + mod_ref[0, 2:3, :] * acc
    if final:
        xn = xn * _rms(xn, xn.shape[-1]) * fw_ref[...]
    o_ref[0] = xn


def _lat_ctx_specs(width, nlt):
    return [pl.BlockSpec((1, TM, width), lambda b, i: (b, jnp.minimum(i, nlt - 1), 0)),
            pl.BlockSpec((1, TM, width), lambda b, i: (b, jnp.maximum(i - nlt, 0), 0))]


def _outproj(oab, oab_ctx, oc, yf, yb, gates, xsrc, mods, pvec, wout, fw, layer, n_lat, final):
    bsz, t, _ = gates.shape
    d = xsrc[0].shape[-1]
    nlt = n_lat // TM
    rows = n_lat if final else t
    with_ctx = oab_ctx is not None
    split_x = len(xsrc) == 2
    tok = lambda w: pl.BlockSpec((1, TM, w), lambda b, i: (b, i, 0))
    layer_blk = lambda a: pl.BlockSpec((None,) + a.shape[1:], lambda b, i: (layer,) + (0,) * (a.ndim - 1))
    oab_specs = _lat_ctx_specs(4 * LANES, nlt) if with_ctx else [tok(4 * LANES)]
    x_specs = _lat_ctx_specs(d, nlt) if split_x else [tok(d)]
    return pl.pallas_call(
        functools.partial(_outproj_kernel, final=final, lat_tiles=nlt, with_ctx=with_ctx, split_x=split_x),
        grid=(bsz, rows // TM),
        in_specs=oab_specs + [tok(2 * LANES), tok(SSD_INNER), tok(SSD_INNER),
                              pl.BlockSpec((1, TM, SSD_INNER), lambda b, i: (b, i, 3))] + x_specs
        + [pl.BlockSpec((None, 1, 3, d), lambda b, i: (layer, jnp.where(i < nlt, b, bsz), 0, 0)),
           layer_blk(pvec), layer_blk(wout), pl.BlockSpec(fw.shape, lambda b, i: (0, 0))],
        out_specs=tok(d),
        out_shape=jax.ShapeDtypeStruct((bsz, rows, d), F32),
        compiler_params=_cparams(("parallel", "parallel")),
        name="outproj",
    )(oab, *([oab_ctx] if with_ctx else []), oc, yf, yb, gates, *xsrc, mods, pvec, wout, fw)


def _rope_tables(n_ctx, n_lat):
    rows = n_lat // GRID_W
    row = jnp.repeat(jnp.arange(rows, dtype=F32), GRID_W)
    col = jnp.tile(jnp.arange(GRID_W, dtype=F32), rows)

    def cos_sin(rot_dim):
        n_freq = rot_dim // 4
        inv_freq = ROPE_THETA ** (-jnp.arange(n_freq, dtype=F32) / n_freq)
        ang = jnp.concatenate([row[:, None] * inv_freq, col[:, None] * inv_freq], axis=-1)
        return jnp.cos(ang), jnp.sin(ang)

    def with_ctx(tab, fill):
        return jnp.concatenate([tab, jnp.full((n_ctx, LANES), fill, F32)], axis=0)

    lane = np.arange(LANES)
    cos_a, sin_a = cos_sin(MLA_ROPE)
    half = MLA_ROPE // 2
    idx = np.where((lane >= MLA_NOPE) & (lane < MLA_NOPE + MLA_ROPE), (lane - MLA_NOPE) % half, 0)
    first = (lane >= MLA_NOPE) & (lane < MLA_NOPE + half)
    second = (lane >= MLA_NOPE + half) & (lane < MLA_NOPE + MLA_ROPE)
    rot = first | second
    ca = jnp.where(rot, cos_a[:, idx], 1.0)
    sa_up = jnp.where(first, -sin_a[:, idx], 0.0)
    sa_dn = jnp.where(second, sin_a[:, idx], 0.0)
    cos_h, sin_h = cos_sin(HEAD_DIM)
    half = HEAD_DIM // 2
    idx = lane % half
    first = (lane % HEAD_DIM) < half
    cb = cos_h[:, idx]
    sb_up = jnp.where(first, -sin_h[:, idx], 0.0)
    sb_dn = jnp.where(first, 0.0, sin_h[:, idx])
    return jnp.stack([with_ctx(ca, 1.0), with_ctx(sa_up, 0.0), with_ctx(sa_dn, 0.0),
                      with_ctx(cb, 1.0), with_ctx(sb_up, 0.0), with_ctx(sb_dn, 0.0)])


def _relayout_w_in(w_in):
    depth, d, _ = w_in.shape
    kr0 = MLA_Q_RANK + MLA_KV_RANK
    dt0 = w_in.shape[-1] - 2 * SSD_HEADS
    cat = jnp.zeros((depth, d, C_END), BF16)
    for src0, src1, dst in ((0, kr0, 0), (dt0, dt0 + 2 * SSD_HEADS, C_DT), (kr0, kr0 + MLA_ROPE, C_KR + MLA_NOPE),
                            (kr0 + MLA_ROPE, dt0, C_GA)):
        cat = lax.dynamic_update_slice(cat, w_in[..., src0:src1].astype(BF16), (0, 0, dst))
    return cat


def _relayout_mla(w_q_up, w_kv_up):
    depth = w_q_up.shape[0]
    wq = w_q_up.reshape(depth, MLA_Q_RANK, MLA_HEADS, MLA_NOPE + MLA_ROPE)
    wq = jnp.pad(wq, ((0, 0), (0, 0), (0, 0), (0, LANES - MLA_NOPE - MLA_ROPE)))
    wq = wq.reshape(depth, MLA_Q_RANK, MLA_HEADS * LANES)
    wkv = w_kv_up.reshape(depth, MLA_KV_RANK, MLA_HEADS, MLA_NOPE + MLA_V)
    wk = jnp.pad(wkv[..., :MLA_NOPE], ((0, 0), (0, 0), (0, 0), (0, LANES - MLA_NOPE)))
    wk = wk.reshape(depth, MLA_KV_RANK, MLA_HEADS * LANES)
    wv = wkv[..., MLA_NOPE:].reshape(depth, MLA_KV_RANK, MLA_HEADS * MLA_V)
    return wq.astype(BF16), jnp.concatenate([wk, wv], axis=-1).astype(BF16)


def _pad_lanes(v, width):
    return jnp.pad(v, ((0, 0), (0, width - v.shape[-1])))


def kernel(x, c, ctx, c_ctx, norm_w, w_mod, b_mod, w_in, mla_q_norm, mla_kv_norm, mla_w_q_up, mla_w_kv_up,
           gqa_q_norm, gqa_k_norm, swa_sink, ssd_conv_w, ssd_conv_b, ssd_a_log, ssd_dt_bias, ssd_d,
           ssd_norm_w, w_out, final_norm_w):
    bsz, n_lat, d = x.shape
    n_ctx = ctx.shape[1]
    depth = w_in.shape[0]
    assert d == 1024 and bsz % NB_IN == 0 and bsz + NB_IN <= 8
    assert n_ctx % TM == 0 and n_ctx % TQ_CTX == 0 and n_ctx % TQ_SWA == 0 and n_ctx % SSD_Q == 0
    assert n_lat % TQ == 0 and n_lat % n_ctx == 0 and n_lat % GRID_W == 0
    assert n_lat >= TQ_SWA + 2 * WINDOW

    cvec = jnp.concatenate([c, jnp.broadcast_to(c_ctx, (NB_IN, d)), jnp.zeros((8 - bsz - NB_IN, d), F32)], axis=0)
    mods = _modulation(cvec, w_mod, b_mod).reshape(depth, 8, 3, d)[:, :bsz + NB_IN]

    wcat = _relayout_w_in(w_in)
    wq, wkv = _relayout_mla(mla_w_q_up, mla_w_kv_up)
    wout = w_out.astype(BF16)
    ropes = _rope_tables(n_ctx, n_lat)
    pv_in = jnp.stack([mla_q_norm,
                       _pad_lanes(mla_kv_norm, 256),
                       jnp.tile(gqa_q_norm, (1, 4)),
                       _pad_lanes(jnp.tile(gqa_k_norm, (1, 2)), 256),
                       _pad_lanes(ssd_dt_bias.reshape(depth, -1), 256)], axis=1)
    pv_in = jnp.pad(pv_in, ((0, 0), (0, 3), (0, 0)))
    pv_out = jnp.stack([jnp.repeat(ssd_d, SSD_HEAD_DIM, axis=-1), ssd_norm_w], axis=1)
    pv_out = jnp.pad(pv_out, ((0, 0), (0, 6), (0, 0)))
    sink_rows = jnp.broadcast_to(jnp.pad(swa_sink, ((0, 0), (0, 4)))[:, :, None], (depth, 8, LANES))
    conv_w8 = jnp.pad(ssd_conv_w, ((0, 0), (0, 8 - SSD_CONV), (0, 0)))
    alog = _pad_lanes(ssd_a_log.reshape(depth, -1), LANES).reshape(depth, 1, LANES)
    conv_b = ssd_conv_b.reshape(depth, 1, -1)
    nw = norm_w.reshape(depth, 1, d)
    fw = final_norm_w.reshape(1, d)

    xsrc = (x, ctx)
    for l in range(depth):
        final = l == depth - 1
        qall, kall, vall, swa_in, gates, xbc, dt = _inproj(xsrc, mods, nw, wcat, wq, wkv, pv_in, ropes, l, n_lat)
        oab = _attention_latent(qall, kall, vall, gates, n_lat)
        oab_ctx = None if final else _attention_ctx(qall, kall, vall, gates, n_lat, n_ctx)
        oc = _swa(swa_in, sink_rows, gates, l, n_lat, n_ctx)
        u = _conv(xbc, conv_w8, conv_b, l, n_lat)
        yf, yb = _ssd(u, dt, alog, pv_out, l, n_lat)
        xsrc = (_outproj(oab, oab_ctx, oc, yf, yb, gates, xsrc, mods, pv_out, wout, fw, l, n_lat, final),)
    return xsrc[0]
```

```python
import functools

import numpy as np
import jax
import jax.numpy as jnp
from jax import lax
from jax.experimental import pallas as pl
from jax.experimental.pallas import tpu as pltpu

F32 = jnp.float32
BF16 = jnp.bfloat16

EPS = 1e-6
GRID_W = 64
ROPE_THETA = 10000.0
HEAD_DIM = 64
MLA_HEADS, MLA_NOPE, MLA_ROPE, MLA_V = 4, 64, 32, 64
MLA_Q_RANK, MLA_KV_RANK = 256, 128
GQA_HEADS, GQA_KV_HEADS = 4, 2
SWA_HEADS, SWA_KV_HEADS = 4, 2
WINDOW = 128
SSD_HEADS, SSD_HEAD_DIM, SSD_GROUPS, SSD_STATE, SSD_CONV = 4, 64, 2, 128, 5
SSD_INNER = SSD_HEADS * SSD_HEAD_DIM
SSD_CONV_DIM = SSD_INNER + 2 * SSD_GROUPS * SSD_STATE

LANES = 128
LOG2E = 1.4426950408889634
NEG_BIG = -1e30

C_CQ, C_CKV, C_KR, C_GA = 0, 256, 384, 512
C_QB, C_KB, C_VB, C_GB = 768, 1024, 1152, 1280
C_QC, C_KC, C_VC, C_GC = 1536, 1792, 1920, 2048
C_Z, C_XBC, C_END = 2304, 2560, 3328
C_DT = C_KR

TM = 256
NB_IN = 2
NB_SWA = 2
NB_SSD = 2
TQ = 1024
TQ_CTX = 256
KV_CHUNK = 1024
TQ_SWA = 256
SSD_Q = 256
HALO = 8
VMEM_LIMIT = 56 * 1024 * 1024


def _cparams(sem):
    return pltpu.CompilerParams(dimension_semantics=sem, vmem_limit_bytes=VMEM_LIMIT)


def _sigmoid(x):
    return 1.0 / (1.0 + jnp.exp(-x))


def _silu(x):
    return x * _sigmoid(x)


def _rms(x, n):
    return lax.rsqrt(jnp.sum(x * x, axis=-1, keepdims=True) * (1.0 / n) + EPS)


def _dot(a, b):
    return jnp.dot(a, b, preferred_element_type=F32)


def _dot_nt(a, b):
    return lax.dot_general(a, b, (((1,), (1,)), ((), ())), preferred_element_type=F32)


def _dot_tn(a, b):
    return lax.dot_general(a, b, (((0,), (0,)), ((), ())), preferred_element_type=F32)


def _lane_iota(rows):
    return lax.broadcasted_iota(jnp.int32, (rows, LANES), 1)


def _mod_kernel(c_ref, w_ref, b_ref, o_ref):
    cs = _silu(c_ref[...])
    o_ref[0] = _dot(cs.astype(BF16), w_ref[0].astype(BF16)) + b_ref[0]


def _modulation(cvec, w_mod, b_mod):
    depth, d, d3 = w_mod.shape
    tn = 512
    return pl.pallas_call(
        _mod_kernel,
        grid=(depth, d3 // tn),
        in_specs=[pl.BlockSpec((8, d), lambda l, j: (0, 0)),
                  pl.BlockSpec((1, d, tn), lambda l, j: (l, 0, j)),
                  pl.BlockSpec((1, 1, tn), lambda l, j: (l, 0, j))],
        out_specs=pl.BlockSpec((1, 8, tn), lambda l, j: (l, 0, j)),
        out_shape=jax.ShapeDtypeStruct((depth, 8, d3), F32),
        compiler_params=_cparams(("parallel", "parallel")),
        name="modulation",
    )(cvec, w_mod, b_mod.reshape(depth, 1, d3))


def _rope(x, c, s_up, s_dn, half):
    return x * c + pltpu.roll(x, LANES - half, 1) * s_up + pltpu.roll(x, half, 1) * s_dn


def _head_rms(blk, lane):
    sq = blk * blk
    lo = lane < HEAD_DIM
    s_lo = jnp.sum(jnp.where(lo, sq, 0.0), axis=-1, keepdims=True)
    s_hi = jnp.sum(jnp.where(lo, 0.0, sq), axis=-1, keepdims=True)
    r_lo = lax.rsqrt(s_lo * (1.0 / HEAD_DIM) + EPS)
    r_hi = lax.rsqrt(s_hi * (1.0 / HEAD_DIM) + EPS)
    return jnp.where(lo, r_lo, r_hi)


def _dup_halves(blk, lane):
    sw = pltpu.roll(blk, HEAD_DIM, 1)
    lo = lane < HEAD_DIM
    return jnp.where(lo, blk, sw), jnp.where(lo, sw, blk)


def _store_v_slots(v_ref, bb, slot, pair, lo):
    v_ref[bb, :, slot * LANES:(slot + 1) * LANES] = jnp.where(lo, pair, 1.0).astype(BF16)
    v_ref[bb, :, (slot + 1) * LANES:(slot + 2) * LANES] = jnp.where(lo, 1.0, pair).astype(BF16)


def _inproj_kernel(*refs, lat_tiles, split_x):
    refs = list(refs)
    x_ref = refs.pop(0)
    xctx_ref = refs.pop(0) if split_x else None
    (mod_ref, nw_ref, wcat_ref, wq_ref, wkv_ref, pv_ref, rope_ref,
     qall_ref, kall_ref, vall_ref, swa_ref, g_ref, xbc_ref, dt_ref) = refs
    nb, tm, d = x_ref.shape
    lane = _lane_iota(tm)
    lo = lane < HEAD_DIM
    is_ctx = pl.program_id(1) >= lat_tiles

    hs = []
    for bb in range(nb):
        x = jnp.where(is_ctx, xctx_ref[bb], x_ref[bb]) if split_x else x_ref[bb]
        xn = x * _rms(x, d)
        h = (xn * nw_ref[...]) * (1.0 + mod_ref[bb, 1:2, :]) + mod_ref[bb, 0:1, :]
        hs.append(h.astype(BF16))
    hcat = jnp.concatenate(hs, axis=0)

    bounds = (C_CQ, C_QB, C_QC, C_Z, C_END)
    proj = {}

    def project(gi):
        proj[gi] = _dot(hcat, wcat_ref[:, bounds[gi]:bounds[gi + 1]])

    def cols(bb, c0, width):
        gi = max(i for i in range(4) if bounds[i] <= c0)
        off = c0 - bounds[gi]
        rows = slice(None) if bb is None else slice(bb * tm, (bb + 1) * tm)
        return proj[gi][rows, off:off + width]

    ca, sa_up, sa_dn = rope_ref[0], rope_ref[1], rope_ref[2]
    cb, sb_up, sb_dn = rope_ref[3], rope_ref[4], rope_ref[5]
    qscale_a = (MLA_NOPE + MLA_ROPE) ** -0.5 * LOG2E
    qscale_h = HEAD_DIM ** -0.5 * LOG2E
    base = MLA_HEADS * LANES

    def stage_a():
        cq = cols(None, C_CQ, MLA_Q_RANK)
        cqn = cq * _rms(cq, MLA_Q_RANK) * pv_ref[0:1, :]
        qa_all = _dot(cqn.astype(BF16), wq_ref[...])
        ckv = cols(None, C_CKV, MLA_KV_RANK)
        ckvn = ckv * _rms(ckv, MLA_KV_RANK) * pv_ref[1:2, 0:MLA_KV_RANK]
        kv_all = _dot(ckvn.astype(BF16), wkv_ref[...])
        for bb in range(nb):
            rows = slice(bb * tm, (bb + 1) * tm)
            qa, kv = qa_all[rows], kv_all[rows]
            kr = jnp.where(lane >= MLA_NOPE, _rope(cols(bb, C_KR, LANES), ca, sa_up, sa_dn, MLA_ROPE // 2), 0.0)
            for hh in range(MLA_HEADS):
                sl = slice(hh * LANES, (hh + 1) * LANES)
                qall_ref[bb, :, sl] = (_rope(qa[:, sl], ca, sa_up, sa_dn, MLA_ROPE // 2) * qscale_a).astype(BF16)
                kall_ref[bb, :, sl] = (kv[:, sl] + kr).astype(BF16)
            for pr in range(2):
                _store_v_slots(vall_ref, bb, 2 * pr, kv[:, (4 + pr) * LANES:(5 + pr) * LANES], lo)
            g_ref[bb, :, 0:256] = _silu(cols(bb, C_GA, 256)).astype(BF16)

    def stage_b(bb):
        for blk_i in range(2):
            qb = cols(bb, C_QB + blk_i * LANES, LANES)
            qb = qb * _head_rms(qb, lane) * pv_ref[2:3, 0:LANES]
            qb = _rope(qb, cb, sb_up, sb_dn, HEAD_DIM // 2) * qscale_h
            c0 = base + 2 * blk_i * LANES
            qall_ref[bb, :, c0:c0 + LANES] = jnp.where(lo, qb, 0.0).astype(BF16)
            qall_ref[bb, :, c0 + LANES:c0 + 2 * LANES] = jnp.where(lo, 0.0, qb).astype(BF16)
        kb = cols(bb, C_KB, LANES)
        kb = kb * _head_rms(kb, lane) * pv_ref[3:4, 0:LANES]
        kb = _rope(kb, cb, sb_up, sb_dn, HEAD_DIM // 2)
        k0, k1 = _dup_halves(kb, lane)
        k0 = k0.astype(BF16)
        k1 = k1.astype(BF16)
        kall_ref[bb, :, base:base + LANES] = k0
        kall_ref[bb, :, base + LANES:base + 2 * LANES] = k0
        kall_ref[bb, :, base + 2 * LANES:base + 3 * LANES] = k1
        kall_ref[bb, :, base + 3 * LANES:base + 4 * LANES] = k1
        v0, v1 = _dup_halves(cols(bb, C_VB, LANES), lane)
        _store_v_slots(vall_ref, bb, 4, v0, lo)
        _store_v_slots(vall_ref, bb, 6, v1, lo)
        g_ref[bb, :, 256:512] = _silu(cols(bb, C_GB, 256)).astype(BF16)

    def stage_c(bb):
        for blk_i in range(2):
            qc = cols(bb, C_QC + blk_i * LANES, LANES)
            swa_ref[bb, :, blk_i * LANES:(blk_i + 1) * LANES] = (
                _rope(qc, cb, sb_up, sb_dn, HEAD_DIM // 2) * qscale_h).astype(BF16)
        kc = _rope(cols(bb, C_KC, LANES), cb, sb_up, sb_dn, HEAD_DIM // 2)
        kc0, kc1 = _dup_halves(kc, lane)
        swa_ref[bb, :, 2 * LANES:3 * LANES] = kc0.astype(BF16)
        swa_ref[bb, :, 3 * LANES:4 * LANES] = kc1.astype(BF16)
        vc0, vc1 = _dup_halves(cols(bb, C_VC, LANES), lane)
        _store_v_slots(swa_ref, bb, 4, vc0, lo)
        _store_v_slots(swa_ref, bb, 6, vc1, lo)
        g_ref[bb, :, 512:768] = _silu(cols(bb, C_GC, 256)).astype(BF16)

    def stage_d(bb):
        g_ref[bb, :, 768:1024] = _silu(cols(bb, C_Z, 256)).astype(BF16)
        xbc_ref[bb] = cols(bb, C_XBC, SSD_CONV_DIM)
        dtr = cols(bb, C_DT, LANES) + pv_ref[4:5, 0:LANES]
        dt_ref[bb] = jnp.maximum(dtr, 0.0) + jnp.log(1.0 + jnp.exp(-jnp.abs(dtr)))

    project(0)
    project(1)
    stage_a()
    project(2)
    for bb in range(nb):
        stage_b(bb)
    project(3)
    for bb in range(nb):
        stage_c(bb)
    for bb in range(nb):
        stage_d(bb)


def _inproj(xsrc, mods, nw, wcat, wq, wkv, pvec, ropes, layer, n_lat):
    bsz, _, d = xsrc[0].shape
    t = ropes.shape[1]
    nlt = n_lat // TM
    split_x = len(xsrc) == 2
    grid = (bsz // NB_IN, t // TM)
    tok = lambda w: pl.BlockSpec((NB_IN, TM, w), lambda b, i: (b, i, 0))
    layer_blk = lambda a: pl.BlockSpec((None,) + a.shape[1:], lambda b, i: (layer,) + (0,) * (a.ndim - 1))
    x_specs = [tok(d)]
    if split_x:
        x_specs = [pl.BlockSpec((NB_IN, TM, d), lambda b, i: (b, jnp.minimum(i, nlt - 1), 0)),
                   pl.BlockSpec((NB_IN, TM, d), lambda b, i: (b, jnp.maximum(i - nlt, 0), 0))]
    widths = (8 * LANES, 8 * LANES, 8 * LANES, 8 * LANES, 1024, SSD_CONV_DIM, LANES)
    dtypes = (BF16, BF16, BF16, BF16, BF16, F32, F32)
    return pl.pallas_call(
        functools.partial(_inproj_kernel, lat_tiles=nlt, split_x=split_x),
        grid=grid,
        in_specs=x_specs + [
            pl.BlockSpec((None, NB_IN, 3, d), lambda b, i: (layer, jnp.where(i < nlt, b, bsz // NB_IN), 0, 0)),
            layer_blk(nw), layer_blk(wcat), layer_blk(wq), layer_blk(wkv), layer_blk(pvec),
            pl.BlockSpec((6, TM, LANES), lambda b, i: (0, i, 0))],
        out_specs=[tok(w) for w in widths],
        out_shape=[jax.ShapeDtypeStruct((bsz, t, w), dt) for w, dt in zip(widths, dtypes)],
        compiler_params=_cparams(("parallel", "parallel")),
        name="inproj",
    )(*xsrc, mods, nw, wcat, wq, wkv, pvec, ropes)


def _kv_chunks(nk):
    return [(c0, min(KV_CHUNK, nk - c0)) for c0 in range(0, nk, KV_CHUNK)]


def _logits_pass(q, k_ref, s_ref, sl, nk):
    mrun = None
    for c0, cw in _kv_chunks(nk):
        s = _dot_nt(q, k_ref[0, c0:c0 + cw, sl])
        s_ref[:, c0:c0 + cw] = s
        for t0 in range(0, cw, LANES):
            piece = s[:, t0:t0 + LANES]
            mrun = piece if mrun is None else jnp.maximum(mrun, piece)
    return jnp.max(mrun, axis=-1, keepdims=True)


def _pv_pass(m, v_ref, s_ref, sl, nk):
    acc = None
    for c0, cw in _kv_chunks(nk):
        pr = jnp.exp2(s_ref[:, c0:c0 + cw] - m).astype(BF16)
        o = _dot(pr, v_ref[0, c0:c0 + cw, sl])
        acc = o if acc is None else acc + o
    return acc / pltpu.roll(acc, HEAD_DIM, 1)


def _attn_body(q_ref, k_ref, v_ref, g_ref, o_ref, s0_ref, s1_ref):
    tq = q_ref.shape[1]
    nk = k_ref.shape[1]
    lane = _lane_iota(tq)
    sls = (slice(0, LANES), slice(LANES, 2 * LANES))
    s_refs = (s0_ref, s1_ref)
    ms = [_logits_pass(q_ref[0, :, sls[j]], k_ref, s_refs[j], sls[j], nk) for j in range(2)]
    outs = [_pv_pass(ms[j], v_ref, s_refs[j], sls[j], nk) for j in range(2)]
    o_ref[0] = (jnp.where(lane < HEAD_DIM, outs[0], outs[1]) * g_ref[0].astype(F32)).astype(o_ref.dtype)


def _attention_latent(qall, kall, vall, gates, n_lat):
    bsz, t, _ = qall.shape
    return pl.pallas_call(
        _attn_body,
        grid=(bsz, 4, n_lat // TQ),
        in_specs=[pl.BlockSpec((1, TQ, 2 * LANES), lambda b, p, i: (b, i, p)),
                  pl.BlockSpec((1, t, 2 * LANES), lambda b, p, i: (b, 0, p)),
                  pl.BlockSpec((1, t, 2 * LANES), lambda b, p, i: (b, 0, p)),
                  pl.BlockSpec((1, TQ, LANES), lambda b, p, i: (b, i, p))],
        out_specs=pl.BlockSpec((1, TQ, LANES), lambda b, p, i: (b, i, p)),
        out_shape=jax.ShapeDtypeStruct((bsz, n_lat, 4 * LANES), BF16),
        scratch_shapes=[pltpu.VMEM((TQ, t), F32), pltpu.VMEM((TQ, t), F32)],
        compiler_params=_cparams(("parallel", "parallel", "arbitrary")),
        name="attn_global",
    )(qall, kall, vall, gates)


def _attention_ctx(qall, kall, vall, gates, n_lat, n_ctx):
    bsz, t, _ = qall.shape
    qoff = n_lat // TQ_CTX
    koff = n_lat // n_ctx
    return pl.pallas_call(
        _attn_body,
        grid=(bsz, 4, n_ctx // TQ_CTX),
        in_specs=[pl.BlockSpec((1, TQ_CTX, 2 * LANES), lambda b, p, i: (b, qoff + i, p)),
                  pl.BlockSpec((1, n_ctx, 2 * LANES), lambda b, p, i: (b, koff, p)),
                  pl.BlockSpec((1, n_ctx, 2 * LANES), lambda b, p, i: (b, koff, p)),
                  pl.BlockSpec((1, TQ_CTX, LANES), lambda b, p, i: (b, qoff + i, p))],
        out_specs=pl.BlockSpec((1, TQ_CTX, LANES), lambda b, p, i: (b, i, p)),
        out_shape=jax.ShapeDtypeStruct((bsz, n_ctx, 4 * LANES), BF16),
        scratch_shapes=[pltpu.VMEM((TQ_CTX, n_ctx), F32), pltpu.VMEM((TQ_CTX, n_ctx), F32)],
        compiler_params=_cparams(("parallel", "parallel", "arbitrary")),
        name="attn_global_ctx",
    )(qall, kall, vall, gates)


def _swa_kernel(q_ref, k_ref, v_ref, bias_ref, sink_ref, g_ref, o_ref, *, n_lat, n_ctx):
    nb, tq = q_ref.shape[0], q_ref.shape[1]
    band = tq + 2 * WINDOW
    i = pl.program_id(1)
    lane = _lane_iota(tq)
    lo = lane < HEAD_DIM

    start = pl.multiple_of(jnp.clip(i * tq - WINDOW, 0, n_lat - band), WINDOW)
    bias = bias_ref[0]
    ks = [jnp.concatenate([k_ref[bb, n_lat:n_lat + n_ctx, :], k_ref[bb, pl.ds(start, band), :]], axis=0)
          for bb in range(nb)]
    vs = [jnp.concatenate([v_ref[bb, n_lat:n_lat + n_ctx, :], v_ref[bb, pl.ds(start, band), :]], axis=0)
          for bb in range(nb)]
    streams = [(bb, hh) for bb in range(nb) for hh in range(SWA_HEADS)]

    def logits_of(bb, hh):
        ps = slice((hh // 2) * LANES, (hh // 2 + 1) * LANES)
        q = q_ref[bb, :, ps]
        zero = jnp.zeros_like(q)
        qj = jnp.where(lo, q, zero) if hh % 2 == 0 else jnp.where(lo, zero, q)
        s = _dot_nt(qj, ks[bb][:, ps]) + bias
        sink = sink_ref[hh:hh + 1, 0:1] * LOG2E
        return s, sink, jnp.maximum(jnp.max(s, axis=-1, keepdims=True), sink)

    ahead = 2
    logits = [logits_of(*streams[j]) for j in range(ahead)]
    outs = {}
    for j, (bb, hh) in enumerate(streams):
        if j + ahead < len(streams):
            logits.append(logits_of(*streams[j + ahead]))
        s, sink, m = logits[j]
        logits[j] = None
        acc = _dot(jnp.exp2(s - m).astype(BF16), vs[bb][:, hh * LANES:(hh + 1) * LANES])
        outs[bb, hh] = acc / (pltpu.roll(acc, HEAD_DIM, 1) + jnp.exp2(sink - m))
    for bb in range(nb):
        for pair in range(2):
            ps = slice(pair * LANES, (pair + 1) * LANES)
            o_ref[bb, :, ps] = (jnp.where(lo, outs[bb, 2 * pair], outs[bb, 2 * pair + 1])
                                * g_ref[bb, :, ps].astype(F32)).astype(o_ref.dtype)


def _swa_bias(n_ctx):
    band = TQ_SWA + 2 * WINDOW
    r = np.arange(TQ_SWA)[:, None]
    c = np.arange(band)[None, :]
    ok = [np.abs(c - shift - r) <= WINDOW for shift in (0, WINDOW, 2 * WINDOW)] + [np.zeros((TQ_SWA, band), bool)]
    lat = np.where(np.stack(ok), 0.0, NEG_BIG).astype(np.float32)
    return jnp.asarray(np.concatenate([np.zeros((4, TQ_SWA, n_ctx), np.float32), lat], axis=-1))


def _swa(swa_in, sink_rows, gates, layer, n_lat, n_ctx):
    bsz, t, _ = swa_in.shape
    nlt = n_lat // TQ_SWA
    bias = _swa_bias(n_ctx)

    def kind(i):
        return jnp.where(i >= nlt, 3, jnp.where(i == 0, 0, jnp.where(i == nlt - 1, 2, 1)))

    return pl.pallas_call(
        functools.partial(_swa_kernel, n_lat=n_lat, n_ctx=n_ctx),
        grid=(bsz // NB_SWA, t // TQ_SWA),
        in_specs=[pl.BlockSpec((NB_SWA, TQ_SWA, 2 * LANES), lambda b, i: (b, i, 0)),
                  pl.BlockSpec((NB_SWA, t, 2 * LANES), lambda b, i: (b, 0, 1)),
                  pl.BlockSpec((NB_SWA, t, 4 * LANES), lambda b, i: (b, 0, 1)),
                  pl.BlockSpec((1,) + bias.shape[1:], lambda b, i: (kind(i), 0, 0)),
                  pl.BlockSpec((None,) + sink_rows.shape[1:], lambda b, i: (layer, 0, 0)),
                  pl.BlockSpec((NB_SWA, TQ_SWA, 2 * LANES), lambda b, i: (b, i, 2))],
        out_specs=pl.BlockSpec((NB_SWA, TQ_SWA, 2 * LANES), lambda b, i: (b, i, 0)),
        out_shape=jax.ShapeDtypeStruct((bsz, t, 2 * LANES), BF16),
        compiler_params=_cparams(("parallel", "arbitrary")),
        name="attn_window",
    )(swa_in, swa_in, swa_in, bias, sink_rows, gates)


def _conv_kernel(x_ref, prev_ref, next_ref, w_ref, b_ref, u_ref, *, lat_tiles):
    tm = x_ref.shape[1]
    i = pl.program_id(1)
    first = (i == 0) | (i == lat_tiles)
    last = (i == lat_tiles - 1) | (i == pl.num_programs(1) - 1)
    ext = jnp.concatenate([jnp.where(first, 0.0, prev_ref[0]), x_ref[0], jnp.where(last, 0.0, next_ref[0])], axis=0)
    n_ext = tm + 2 * HALO
    acc = x_ref[0] * w_ref[SSD_CONV // 2:SSD_CONV // 2 + 1, :] + b_ref[...]
    for tap in range(SSD_CONV):
        if tap != SSD_CONV // 2:
            shifted = pltpu.roll(ext, (SSD_CONV // 2 - tap) % n_ext, 0)
            acc = acc + shifted[HALO:HALO + tm, :] * w_ref[tap:tap + 1, :]
    u_ref[0] = _silu(acc).astype(u_ref.dtype)


def _conv(xbc, conv_w8, conv_b, layer, n_lat):
    bsz, t, c = xbc.shape
    per = TM // HALO
    nh = t // HALO
    return pl.pallas_call(
        functools.partial(_conv_kernel, lat_tiles=n_lat // TM),
        grid=(bsz, t // TM),
        in_specs=[pl.BlockSpec((1, TM, c), lambda b, i: (b, i, 0)),
                  pl.BlockSpec((1, HALO, c), lambda b, i: (b, jnp.maximum(i * per - 1, 0), 0)),
                  pl.BlockSpec((1, HALO, c), lambda b, i: (b, jnp.minimum((i + 1) * per, nh - 1), 0)),
                  pl.BlockSpec((None,) + conv_w8.shape[1:], lambda b, i: (layer, 0, 0)),
                  pl.BlockSpec((None,) + conv_b.shape[1:], lambda b, i: (layer, 0, 0))],
        out_specs=pl.BlockSpec((1, TM, c), lambda b, i: (b, i, 0)),
        out_shape=jax.ShapeDtypeStruct((bsz, t, c), BF16),
        compiler_params=_cparams(("parallel", "parallel")),
        name="ssd_conv",
    )(xbc, xbc, xbc, conv_w8, conv_b)


def _ssd_kernel(uf_ref, ub_ref, dtf_ref, dtb_ref, alog_ref, pv_ref, yf_ref, yb_ref, st_ref):
    nb, q = uf_ref.shape[0], uf_ref.shape[1]
    u_refs, dt_refs, y_refs = (uf_ref, ub_ref), (dtf_ref, dtb_ref), (yf_ref, yb_ref)

    @pl.when(pl.program_id(1) == 0)
    def _():
        st_ref[...] = jnp.zeros_like(st_ref)

    lane1 = lax.broadcasted_iota(jnp.int32, (1, LANES), 1)
    avec = jnp.where(lane1 < 2 * SSD_HEADS, -jnp.exp(alog_ref[...]), 0.0)
    row = lax.broadcasted_iota(jnp.int32, (q, q), 0)
    col = lax.broadcasted_iota(jnp.int32, (q, q), 1)
    masks = (col <= row, col >= row)
    lo = _lane_iota(q) < SSD_HEAD_DIM
    lo1 = lane1 < SSD_HEAD_DIM
    scans = [(bb, d) for bb in range(nb) for d in range(2)]
    streams = [(bb, d, g) for bb, d in scans for g in range(SSD_GROUPS)]

    us, dts, a_cums, a_cum_ts, a_tots = {}, {}, {}, {}, {}
    for bb, d in scans:
        us[bb, d] = u_refs[d][bb]
        dt = dt_refs[d][bb]
        dta = dt * avec
        ones = masks[d].astype(BF16)
        hi = dta.astype(BF16)
        r1 = dta - hi.astype(F32)
        mid = r1.astype(BF16)
        lo3 = (r1 - mid.astype(F32)).astype(BF16)
        a_cum = _dot(ones, hi) + _dot(ones, mid) + _dot(ones, lo3)
        dts[bb, d] = dt
        a_cums[bb, d] = a_cum
        a_cum_ts[bb, d] = a_cum.T
        a_tots[bb, d] = a_cum[q - 1:q, :] if d == 0 else a_cum[0:1, :]

    def head_col(arr, d, hh):
        c = SSD_HEADS * d + hh
        return arr[:, c:c + 1]

    bgs, cgs, gmats, xdts, acls, atls, xss = {}, {}, {}, {}, {}, {}, {}
    for bb, d, g in streams:
        u = us[bb, d]
        b0 = SSD_INNER + g * SSD_STATE
        c0 = SSD_INNER + (SSD_GROUPS + g) * SSD_STATE
        bgs[bb, d, g] = u[:, b0:b0 + SSD_STATE]
        cgs[bb, d, g] = u[:, c0:c0 + SSD_STATE]
        xss[bb, d, g] = u[:, g * LANES:(g + 1) * LANES].astype(F32)
        gmats[bb, d, g] = _dot_nt(cgs[bb, d, g], bgs[bb, d, g])
    for bb, d, g in streams:
        h0, h1 = 2 * g, 2 * g + 1
        a_cum, a_tot, dt = a_cums[bb, d], a_tots[bb, d], dts[bb, d]
        acls[bb, d, g] = jnp.where(lo, head_col(a_cum, d, h0), head_col(a_cum, d, h1))
        atls[bb, d, g] = jnp.where(lo1, head_col(a_tot, d, h0), head_col(a_tot, d, h1))
        xdts[bb, d, g] = xss[bb, d, g] * jnp.where(lo, head_col(dt, d, h0), head_col(dt, d, h1))

    intra = {}
    for bb, d, g in streams:
        xdt_b = xdts[bb, d, g].astype(BF16)
        ys = []
        for hh in (2 * g, 2 * g + 1):
            c = SSD_HEADS * d + hh
            seg = a_cums[bb, d][:, c:c + 1] - a_cum_ts[bb, d][c:c + 1, :]
            dec = jnp.exp(jnp.where(masks[d], seg, NEG_BIG))
            ys.append(_dot((gmats[bb, d, g] * dec).astype(BF16), xdt_b))
        intra[bb, d, g] = jnp.where(lo, ys[0], ys[1])

    for bb, d, g in streams:
        key = (bb, d, g)
        st = st_ref[bb, d, g]
        y = intra[key] + _dot(cgs[key], st.astype(BF16)) * jnp.exp(acls[key])
        if d == 0:
            y = y + pv_ref[0:1, g * LANES:(g + 1) * LANES] * xss[key]
        y_refs[d][bb, :, g * LANES:(g + 1) * LANES] = y.astype(y_refs[d].dtype)
        w = (xdts[key] * jnp.exp(atls[key] - acls[key])).astype(BF16)
        st_ref[bb, d, g] = st * jnp.exp(atls[key]) + _dot_tn(bgs[key], w)


def _ssd(u, dt, alog, pvec, layer, n_lat):
    bsz, t, c = u.shape
    nc = t // SSD_Q
    ncl = n_lat // SSD_Q
    ncc = nc - ncl

    def fwd(s):
        return jnp.where(s < ncc, ncl + s, s - ncc)

    def bwd(s):
        return jnp.where(s < ncc, nc - 1 - s, ncl - 1 - (s - ncc))

    layer_blk = lambda a: pl.BlockSpec((None,) + a.shape[1:], lambda b, s: (layer,) + (0,) * (a.ndim - 1))
    y_shape = jax.ShapeDtypeStruct((bsz, t, SSD_INNER), BF16)
    return pl.pallas_call(
        _ssd_kernel,
        grid=(bsz // NB_SSD, nc),
        in_specs=[pl.BlockSpec((NB_SSD, SSD_Q, c), lambda b, s: (b, fwd(s), 0)),
                  pl.BlockSpec((NB_SSD, SSD_Q, c), lambda b, s: (b, bwd(s), 0)),
                  pl.BlockSpec((NB_SSD, SSD_Q, LANES), lambda b, s: (b, fwd(s), 0)),
                  pl.BlockSpec((NB_SSD, SSD_Q, LANES), lambda b, s: (b, bwd(s), 0)),
                  layer_blk(alog), layer_blk(pvec)],
        out_specs=[pl.BlockSpec((NB_SSD, SSD_Q, SSD_INNER), lambda b, s: (b, fwd(s), 0)),
                   pl.BlockSpec((NB_SSD, SSD_Q, SSD_INNER), lambda b, s: (b, bwd(s), 0))],
        out_shape=[y_shape, y_shape],
        scratch_shapes=[pltpu.VMEM((NB_SSD, 2, SSD_GROUPS, SSD_STATE, LANES), F32)],
        compiler_params=_cparams(("parallel", "arbitrary")),
        name="ssd_scan",
    )(u, u, dt, dt, alog, pvec)


def _outproj_kernel(*refs, final, lat_tiles, with_ctx, split_x):
    refs = list(refs)
    oab_ref = refs.pop(0)
    octx_ref = refs.pop(0) if with_ctx else None
    oc_ref, yf_ref, yb_ref, g_ref, x_ref = refs[:5]
    del refs[:5]
    xctx_ref = refs.pop(0) if split_x else None
    mod_ref, pv_ref, w_ref, fw_ref, o_ref = refs
    is_ctx = pl.program_id(1) >= lat_tiles

    oab = jnp.where(is_ctx, octx_ref[0], oab_ref[0]) if with_ctx else oab_ref[0]
    x = jnp.where(is_ctx, xctx_ref[0], x_ref[0]) if split_x else x_ref[0]
    yd = yf_ref[0].astype(F32) + yb_ref[0].astype(F32)
    td = yd * g_ref[0].astype(F32)
    m_d = td * _rms(td, SSD_INNER) * pv_ref[1:2, :]
    acc = _dot(oab, w_ref[0:512, :])
    acc += _dot(oc_ref[0], w_ref[512:768, :])
    acc += _dot(m_d.astype(BF16), w_ref[768:1024, :])
    xn = x + mod_ref[0, 2:3, :] * acc
    if final:
        xn = xn * _rms(xn, xn.shape[-1]) * fw_ref[...]
    o_ref[0] = xn


def _lat_ctx_specs(width, nlt):
    return [pl.BlockSpec((1, TM, width), lambda b, i: (b, jnp.minimum(i, nlt - 1), 0)),
            pl.BlockSpec((1, TM, width), lambda b, i: (b, jnp.maximum(i - nlt, 0), 0))]


def _outproj(oab, oab_ctx, oc, yf, yb, gates, xsrc, mods, pvec, wout, fw, layer, n_lat, final):
    bsz, t, _ = gates.shape
    d = xsrc[0].shape[-1]
    nlt = n_lat // TM
    rows = n_lat if final else t
    with_ctx = oab_ctx is not None
    split_x = len(xsrc) == 2
    tok = lambda w: pl.BlockSpec((1, TM, w), lambda b, i: (b, i, 0))
    layer_blk = lambda a: pl.BlockSpec((None,) + a.shape[1:], lambda b, i: (layer,) + (0,) * (a.ndim - 1))
    oab_specs = _lat_ctx_specs(4 * LANES, nlt) if with_ctx else [tok(4 * LANES)]
    x_specs = _lat_ctx_specs(d, nlt) if split_x else [tok(d)]
    return pl.pallas_call(
        functools.partial(_outproj_kernel, final=final, lat_tiles=nlt, with_ctx=with_ctx, split_x=split_x),
        grid=(bsz, rows // TM),
        in_specs=oab_specs + [tok(2 * LANES), tok(SSD_INNER), tok(SSD_INNER),
                              pl.BlockSpec((1, TM, SSD_INNER), lambda b, i: (b, i, 3))] + x_specs
        + [pl.BlockSpec((None, 1, 3, d), lambda b, i: (layer, jnp.where(i < nlt, b, bsz), 0, 0)),
           layer_blk(pvec), layer_blk(wout), pl.BlockSpec(fw.shape, lambda b, i: (0, 0))],
        out_specs=tok(d),
        out_shape=jax.ShapeDtypeStruct((bsz, rows, d), F32),
        compiler_params=_cparams(("parallel", "parallel")),
        name="outproj",
    )(oab, *([oab_ctx] if with_ctx else []), oc, yf, yb, gates, *xsrc, mods, pvec, wout, fw)


def _rope_tables(n_ctx, n_lat):
    rows = n_lat // GRID_W
    row = jnp.repeat(jnp.arange(rows, dtype=F32), GRID_W)
    col = jnp.tile(jnp.arange(GRID_W, dtype=F32), rows)

    def cos_sin(rot_dim):
        n_freq = rot_dim // 4
        inv_freq = ROPE_THETA ** (-jnp.arange(n_freq, dtype=F32) / n_freq)
        ang = jnp.concatenate([row[:, None] * inv_freq, col[:, None] * inv_freq], axis=-1)
        return jnp.cos(ang), jnp.sin(ang)

    def with_ctx(tab, fill):
        return jnp.concatenate([tab, jnp.full((n_ctx, LANES), fill, F32)], axis=0)

    lane = np.arange(LANES)
    cos_a, sin_a = cos_sin(MLA_ROPE)
    half = MLA_ROPE // 2
    idx = np.where((lane >= MLA_NOPE) & (lane < MLA_NOPE + MLA_ROPE), (lane - MLA_NOPE) % half, 0)
    first = (lane >= MLA_NOPE) & (lane < MLA_NOPE + half)
    second = (lane >= MLA_NOPE + half) & (lane < MLA_NOPE + MLA_ROPE)
    rot = first | second
    ca = jnp.where(rot, cos_a[:, idx], 1.0)
    sa_up = jnp.where(first, -sin_a[:, idx], 0.0)
    sa_dn = jnp.where(second, sin_a[:, idx], 0.0)
    cos_h, sin_h = cos_sin(HEAD_DIM)
    half = HEAD_DIM // 2
    idx = lane % half
    first = (lane % HEAD_DIM) < half
    cb = cos_h[:, idx]
    sb_up = jnp.where(first, -sin_h[:, idx], 0.0)
    sb_dn = jnp.where(first, 0.0, sin_h[:, idx])
    return jnp.stack([with_ctx(ca, 1.0), with_ctx(sa_up, 0.0), with_ctx(sa_dn, 0.0),
                      with_ctx(cb, 1.0), with_ctx(sb_up, 0.0), with_ctx(sb_dn, 0.0)])


def _relayout_w_in(w_in):
    depth, d, _ = w_in.shape
    kr0 = MLA_Q_RANK + MLA_KV_RANK
    dt0 = w_in.shape[-1] - 2 * SSD_HEADS
    cat = jnp.zeros((depth, d, C_END), BF16)
    for src0, src1, dst in ((0, kr0, 0), (dt0, dt0 + 2 * SSD_HEADS, C_DT), (kr0, kr0 + MLA_ROPE, C_KR + MLA_NOPE),
                            (kr0 + MLA_ROPE, dt0, C_GA)):
        cat = lax.dynamic_update_slice(cat, w_in[..., src0:src1].astype(BF16), (0, 0, dst))
    return cat


def _relayout_mla(w_q_up, w_kv_up):
    depth = w_q_up.shape[0]
    wq = w_q_up.reshape(depth, MLA_Q_RANK, MLA_HEADS, MLA_NOPE + MLA_ROPE)
    wq = jnp.pad(wq, ((0, 0), (0, 0), (0, 0), (0, LANES - MLA_NOPE - MLA_ROPE)))
    wq = wq.reshape(depth, MLA_Q_RANK, MLA_HEADS * LANES)
    wkv = w_kv_up.reshape(depth, MLA_KV_RANK, MLA_HEADS, MLA_NOPE + MLA_V)
    wk = jnp.pad(wkv[..., :MLA_NOPE], ((0, 0), (0, 0), (0, 0), (0, LANES - MLA_NOPE)))
    wk = wk.reshape(depth, MLA_KV_RANK, MLA_HEADS * LANES)
    wv = wkv[..., MLA_NOPE:].reshape(depth, MLA_KV_RANK, MLA_HEADS * MLA_V)
    return wq.astype(BF16), jnp.concatenate([wk, wv], axis=-1).astype(BF16)


def _pad_lanes(v, width):
    return jnp.pad(v, ((0, 0), (0, width - v.shape[-1])))


def kernel(x, c, ctx, c_ctx, norm_w, w_mod, b_mod, w_in, mla_q_norm, mla_kv_norm, mla_w_q_up, mla_w_kv_up,
           gqa_q_norm, gqa_k_norm, swa_sink, ssd_conv_w, ssd_conv_b, ssd_a_log, ssd_dt_bias, ssd_d,
           ssd_norm_w, w_out, final_norm_w):
    bsz, n_lat, d = x.shape
    n_ctx = ctx.shape[1]
    depth = w_in.shape[0]
    assert d == 1024 and bsz % NB_IN == 0 and bsz + NB_IN <= 8 and bsz % NB_SWA == 0 and bsz % NB_SSD == 0
    assert n_ctx % TM == 0 and n_ctx % TQ_CTX == 0 and n_ctx % TQ_SWA == 0 and n_ctx % SSD_Q == 0
    assert n_lat % TQ == 0 and n_lat % n_ctx == 0 and n_lat % GRID_W == 0
    assert n_lat >= TQ_SWA + 2 * WINDOW

    cvec = jnp.concatenate([c, jnp.broadcast_to(c_ctx, (NB_IN, d)), jnp.zeros((8 - bsz - NB_IN, d), F32)], axis=0)
    mods = _modulation(cvec, w_mod, b_mod).reshape(depth, 8, 3, d)[:, :bsz + NB_IN]

    wcat = _relayout_w_in(w_in)
    wq, wkv = _relayout_mla(mla_w_q_up, mla_w_kv_up)
    wout = w_out.astype(BF16)
    ropes = _rope_tables(n_ctx, n_lat)
    pv_in = jnp.stack([mla_q_norm,
                       _pad_lanes(mla_kv_norm, 256),
                       jnp.tile(gqa_q_norm, (1, 4)),
                       _pad_lanes(jnp.tile(gqa_k_norm, (1, 2)), 256),
                       _pad_lanes(ssd_dt_bias.reshape(depth, -1), 256)], axis=1)
    pv_in = jnp.pad(pv_in, ((0, 0), (0, 3), (0, 0)))
    pv_out = jnp.stack([jnp.repeat(ssd_d, SSD_HEAD_DIM, axis=-1), ssd_norm_w], axis=1)
    pv_out = jnp.pad(pv_out, ((0, 0), (0, 6), (0, 0)))
    sink_rows = jnp.broadcast_to(jnp.pad(swa_sink, ((0, 0), (0, 4)))[:, :, None], (depth, 8, LANES))
    conv_w8 = jnp.pad(ssd_conv_w, ((0, 0), (0, 8 - SSD_CONV), (0, 0)))
    alog = _pad_lanes(ssd_a_log.reshape(depth, -1), LANES).reshape(depth, 1, LANES)
    conv_b = ssd_conv_b.reshape(depth, 1, -1)
    nw = norm_w.reshape(depth, 1, d)
    fw = final_norm_w.reshape(1, d)

    xsrc = (x, ctx)
    for l in range(depth):
        final = l == depth - 1
        qall, kall, vall, swa_in, gates, xbc, dt = _inproj(xsrc, mods, nw, wcat, wq, wkv, pv_in, ropes, l, n_lat)
        oab = _attention_latent(qall, kall, vall, gates, n_lat)
        oab_ctx = None if final else _attention_ctx(qall, kall, vall, gates, n_lat, n_ctx)
        oc = _swa(swa_in, sink_rows, gates, l, n_lat, n_ctx)
        u = _conv(xbc, conv_w8, conv_b, l, n_lat)
        yf, yb = _ssd(u, dt, alog, pv_out, l, n_lat)
        xsrc = (_outproj(oab, oab_ctx, oc, yf, yb, gates, xsrc, mods, pv_out, wout, fw, l, n_lat, final),)
    return xsrc[0]
```

```python
import functools

import numpy as np
import jax
import jax.numpy as jnp
from jax import lax
from jax.experimental import pallas as pl
from jax.experimental.pallas import tpu as pltpu

F32 = jnp.float32
BF16 = jnp.bfloat16

EPS = 1e-6
GRID_W = 64
ROPE_THETA = 10000.0
HEAD_DIM = 64
MLA_HEADS, MLA_NOPE, MLA_ROPE, MLA_V = 4, 64, 32, 64
MLA_Q_RANK, MLA_KV_RANK = 256, 128
GQA_HEADS, GQA_KV_HEADS = 4, 2
SWA_HEADS, SWA_KV_HEADS = 4, 2
WINDOW = 128
SSD_HEADS, SSD_HEAD_DIM, SSD_GROUPS, SSD_STATE, SSD_CONV = 4, 64, 2, 128, 5
SSD_INNER = SSD_HEADS * SSD_HEAD_DIM
SSD_CONV_DIM = SSD_INNER + 2 * SSD_GROUPS * SSD_STATE

LANES = 128
LOG2E = 1.4426950408889634
NEG_BIG = -1e30

C_CQ, C_CKV, C_KR, C_GA = 0, 256, 384, 512
C_QB, C_KB, C_VB, C_GB = 768, 1024, 1152, 1280
C_QC, C_KC, C_VC, C_GC = 1536, 1792, 1920, 2048
C_Z, C_XBC, C_END = 2304, 2560, 3328
C_DT = C_KR

TM = 256
NB_IN = 2
NB_SWA = 2
NB_SSD = 2
NB_ROWS = 2
TQ = 1024
TQ_CTX = 256
KV_CHUNK = 1024
TQ_SWA = 256
SSD_Q = 256
HALO = 8
VMEM_LIMIT = 56 * 1024 * 1024


def _cparams(sem):
    return pltpu.CompilerParams(dimension_semantics=sem, vmem_limit_bytes=VMEM_LIMIT)


def _sigmoid(x):
    return 1.0 / (1.0 + jnp.exp(-x))


def _silu(x):
    return x * _sigmoid(x)


def _rms(x, n):
    return lax.rsqrt(jnp.sum(x * x, axis=-1, keepdims=True) * (1.0 / n) + EPS)


def _dot(a, b):
    return jnp.dot(a, b, preferred_element_type=F32)


def _dot_nt(a, b):
    return lax.dot_general(a, b, (((1,), (1,)), ((), ())), preferred_element_type=F32)


def _dot_tn(a, b):
    return lax.dot_general(a, b, (((0,), (0,)), ((), ())), preferred_element_type=F32)


def _lane_iota(rows):
    return lax.broadcasted_iota(jnp.int32, (rows, LANES), 1)


def _mod_kernel(c_ref, w_ref, b_ref, o_ref):
    cs = _silu(c_ref[...])
    o_ref[0] = _dot(cs.astype(BF16), w_ref[0].astype(BF16)) + b_ref[0]


def _modulation(cvec, w_mod, b_mod):
    depth, d, d3 = w_mod.shape
    tn = 512
    return pl.pallas_call(
        _mod_kernel,
        grid=(depth, d3 // tn),
        in_specs=[pl.BlockSpec((8, d), lambda l, j: (0, 0)),
                  pl.BlockSpec((1, d, tn), lambda l, j: (l, 0, j)),
                  pl.BlockSpec((1, 1, tn), lambda l, j: (l, 0, j))],
        out_specs=pl.BlockSpec((1, 8, tn), lambda l, j: (l, 0, j)),
        out_shape=jax.ShapeDtypeStruct((depth, 8, d3), F32),
        compiler_params=_cparams(("parallel", "parallel")),
        name="modulation",
    )(cvec, w_mod, b_mod.reshape(depth, 1, d3))


def _rope(x, c, s_up, s_dn, half):
    return x * c + pltpu.roll(x, LANES - half, 1) * s_up + pltpu.roll(x, half, 1) * s_dn


def _head_rms(blk, lane):
    sq = blk * blk
    lo = lane < HEAD_DIM
    s_lo = jnp.sum(jnp.where(lo, sq, 0.0), axis=-1, keepdims=True)
    s_hi = jnp.sum(jnp.where(lo, 0.0, sq), axis=-1, keepdims=True)
    r_lo = lax.rsqrt(s_lo * (1.0 / HEAD_DIM) + EPS)
    r_hi = lax.rsqrt(s_hi * (1.0 / HEAD_DIM) + EPS)
    return jnp.where(lo, r_lo, r_hi)


def _dup_halves(blk, lane):
    sw = pltpu.roll(blk, HEAD_DIM, 1)
    lo = lane < HEAD_DIM
    return jnp.where(lo, blk, sw), jnp.where(lo, sw, blk)


def _store_v_slots(v_ref, bb, slot, pair, lo):
    v_ref[bb, :, slot * LANES:(slot + 1) * LANES] = jnp.where(lo, pair, 1.0).astype(BF16)
    v_ref[bb, :, (slot + 1) * LANES:(slot + 2) * LANES] = jnp.where(lo, 1.0, pair).astype(BF16)


def _inproj_kernel(*refs, lat_tiles, split_x):
    refs = list(refs)
    x_ref = refs.pop(0)
    xctx_ref = refs.pop(0) if split_x else None
    (mod_ref, nw_ref, wcat_ref, wq_ref, wkv_ref, pv_ref, rope_ref,
     qall_ref, kall_ref, vall_ref, swa_ref, g_ref, xbc_ref, dt_ref) = refs
    nb, tm, d = x_ref.shape
    lane = _lane_iota(tm)
    lo = lane < HEAD_DIM
    is_ctx = pl.program_id(1) >= lat_tiles

    hs = []
    for bb in range(nb):
        x = jnp.where(is_ctx, xctx_ref[bb], x_ref[bb]) if split_x else x_ref[bb]
        xn = x * _rms(x, d)
        h = (xn * nw_ref[...]) * (1.0 + mod_ref[bb, 1:2, :]) + mod_ref[bb, 0:1, :]
        hs.append(h.astype(BF16))
    hcat = jnp.concatenate(hs, axis=0)

    bounds = (C_CQ, C_QB, C_QC, C_Z, C_END)
    proj = {}

    def project(gi):
        proj[gi] = _dot(hcat, wcat_ref[:, bounds[gi]:bounds[gi + 1]])

    def cols(bb, c0, width):
        gi = max(i for i in range(4) if bounds[i] <= c0)
        off = c0 - bounds[gi]
        rows = slice(None) if bb is None else slice(bb * tm, (bb + 1) * tm)
        return proj[gi][rows, off:off + width]

    ca, sa_up, sa_dn = rope_ref[0], rope_ref[1], rope_ref[2]
    cb, sb_up, sb_dn = rope_ref[3], rope_ref[4], rope_ref[5]
    qscale_a = (MLA_NOPE + MLA_ROPE) ** -0.5 * LOG2E
    qscale_h = HEAD_DIM ** -0.5 * LOG2E
    base = MLA_HEADS * LANES

    def stage_a():
        cq = cols(None, C_CQ, MLA_Q_RANK)
        cqn = cq * _rms(cq, MLA_Q_RANK) * pv_ref[0:1, :]
        qa_all = _dot(cqn.astype(BF16), wq_ref[...])
        ckv = cols(None, C_CKV, MLA_KV_RANK)
        ckvn = ckv * _rms(ckv, MLA_KV_RANK) * pv_ref[1:2, 0:MLA_KV_RANK]
        kv_all = _dot(ckvn.astype(BF16), wkv_ref[...])
        for bb in range(nb):
            rows = slice(bb * tm, (bb + 1) * tm)
            qa, kv = qa_all[rows], kv_all[rows]
            kr = jnp.where(lane >= MLA_NOPE, _rope(cols(bb, C_KR, LANES), ca, sa_up, sa_dn, MLA_ROPE // 2), 0.0)
            for hh in range(MLA_HEADS):
                sl = slice(hh * LANES, (hh + 1) * LANES)
                qall_ref[bb, :, sl] = (_rope(qa[:, sl], ca, sa_up, sa_dn, MLA_ROPE // 2) * qscale_a).astype(BF16)
                kall_ref[bb, :, sl] = (kv[:, sl] + kr).astype(BF16)
            for pr in range(2):
                _store_v_slots(vall_ref, bb, 2 * pr, kv[:, (4 + pr) * LANES:(5 + pr) * LANES], lo)
            g_ref[bb, :, 0:256] = _silu(cols(bb, C_GA, 256)).astype(BF16)

    def stage_b(bb):
        for blk_i in range(2):
            qb = cols(bb, C_QB + blk_i * LANES, LANES)
            qb = qb * _head_rms(qb, lane) * pv_ref[2:3, 0:LANES]
            qb = _rope(qb, cb, sb_up, sb_dn, HEAD_DIM // 2) * qscale_h
            c0 = base + 2 * blk_i * LANES
            qall_ref[bb, :, c0:c0 + LANES] = jnp.where(lo, qb, 0.0).astype(BF16)
            qall_ref[bb, :, c0 + LANES:c0 + 2 * LANES] = jnp.where(lo, 0.0, qb).astype(BF16)
        kb = cols(bb, C_KB, LANES)
        kb = kb * _head_rms(kb, lane) * pv_ref[3:4, 0:LANES]
        kb = _rope(kb, cb, sb_up, sb_dn, HEAD_DIM // 2)
        k0, k1 = _dup_halves(kb, lane)
        k0 = k0.astype(BF16)
        k1 = k1.astype(BF16)
        kall_ref[bb, :, base:base + LANES] = k0
        kall_ref[bb, :, base + LANES:base + 2 * LANES] = k0
        kall_ref[bb, :, base + 2 * LANES:base + 3 * LANES] = k1
        kall_ref[bb, :, base + 3 * LANES:base + 4 * LANES] = k1
        v0, v1 = _dup_halves(cols(bb, C_VB, LANES), lane)
        _store_v_slots(vall_ref, bb, 4, v0, lo)
        _store_v_slots(vall_ref, bb, 6, v1, lo)
        g_ref[bb, :, 256:512] = _silu(cols(bb, C_GB, 256)).astype(BF16)

    def stage_c(bb):
        for blk_i in range(2):
            qc = cols(bb, C_QC + blk_i * LANES, LANES)
            swa_ref[bb, :, blk_i * LANES:(blk_i + 1) * LANES] = (
                _rope(qc, cb, sb_up, sb_dn, HEAD_DIM // 2) * qscale_h).astype(BF16)
        kc = _rope(cols(bb, C_KC, LANES), cb, sb_up, sb_dn, HEAD_DIM // 2)
        kc0, kc1 = _dup_halves(kc, lane)
        swa_ref[bb, :, 2 * LANES:3 * LANES] = kc0.astype(BF16)
        swa_ref[bb, :, 3 * LANES:4 * LANES] = kc1.astype(BF16)
        vc0, vc1 = _dup_halves(cols(bb, C_VC, LANES), lane)
        _store_v_slots(swa_ref, bb, 4, vc0, lo)
        _store_v_slots(swa_ref, bb, 6, vc1, lo)
        g_ref[bb, :, 512:768] = _silu(cols(bb, C_GC, 256)).astype(BF16)

    def stage_d(bb):
        g_ref[bb, :, 768:1024] = _silu(cols(bb, C_Z, 256)).astype(BF16)
        xbc_ref[bb] = cols(bb, C_XBC, SSD_CONV_DIM)
        dtr = cols(bb, C_DT, LANES) + pv_ref[4:5, 0:LANES]
        dt_ref[bb] = jnp.maximum(dtr, 0.0) + jnp.log(1.0 + jnp.exp(-jnp.abs(dtr)))

    project(0)
    project(1)
    stage_a()
    project(2)
    for bb in range(nb):
        stage_b(bb)
    project(3)
    for bb in range(nb):
        stage_c(bb)
    for bb in range(nb):
        stage_d(bb)


def _inproj(xsrc, mods, nw, wcat, wq, wkv, pvec, ropes, layer, n_lat):
    bsz, _, d = xsrc[0].shape
    t = ropes.shape[1]
    nlt = n_lat // TM
    split_x = len(xsrc) == 2
    grid = (bsz // NB_IN, t // TM)
    tok = lambda w: pl.BlockSpec((NB_IN, TM, w), lambda b, i: (b, i, 0))
    layer_blk = lambda a: pl.BlockSpec((None,) + a.shape[1:], lambda b, i: (layer,) + (0,) * (a.ndim - 1))
    x_specs = [tok(d)]
    if split_x:
        x_specs = [pl.BlockSpec((NB_IN, TM, d), lambda b, i: (b, jnp.minimum(i, nlt - 1), 0)),
                   pl.BlockSpec((NB_IN, TM, d), lambda b, i: (b, jnp.maximum(i - nlt, 0), 0))]
    widths = (8 * LANES, 8 * LANES, 8 * LANES, 8 * LANES, 1024, SSD_CONV_DIM, LANES)
    dtypes = (BF16, BF16, BF16, BF16, BF16, F32, F32)
    return pl.pallas_call(
        functools.partial(_inproj_kernel, lat_tiles=nlt, split_x=split_x),
        grid=grid,
        in_specs=x_specs + [
            pl.BlockSpec((None, NB_IN, 3, d), lambda b, i: (layer, jnp.where(i < nlt, b, bsz // NB_IN), 0, 0)),
            layer_blk(nw), layer_blk(wcat), layer_blk(wq), layer_blk(wkv), layer_blk(pvec),
            pl.BlockSpec((6, TM, LANES), lambda b, i: (0, i, 0))],
        out_specs=[tok(w) for w in widths],
        out_shape=[jax.ShapeDtypeStruct((bsz, t, w), dt) for w, dt in zip(widths, dtypes)],
        compiler_params=_cparams(("parallel", "parallel")),
        name="inproj",
    )(*xsrc, mods, nw, wcat, wq, wkv, pvec, ropes)


def _kv_chunks(nk):
    return [(c0, min(KV_CHUNK, nk - c0)) for c0 in range(0, nk, KV_CHUNK)]


def _logits_pass(q, k_ref, s_ref, sl, nk):
    mrun = None
    for c0, cw in _kv_chunks(nk):
        s = _dot_nt(q, k_ref[0, c0:c0 + cw, sl])
        s_ref[:, c0:c0 + cw] = s
        for t0 in range(0, cw, LANES):
            piece = s[:, t0:t0 + LANES]
            mrun = piece if mrun is None else jnp.maximum(mrun, piece)
    return jnp.max(mrun, axis=-1, keepdims=True)


def _pv_pass(m, v_ref, s_ref, sl, nk):
    acc = None
    for c0, cw in _kv_chunks(nk):
        pr = jnp.exp2(s_ref[:, c0:c0 + cw] - m).astype(BF16)
        o = _dot(pr, v_ref[0, c0:c0 + cw, sl])
        acc = o if acc is None else acc + o
    return acc / pltpu.roll(acc, HEAD_DIM, 1)


def _attn_body(q_ref, k_ref, v_ref, g_ref, o_ref, s0_ref, s1_ref):
    tq = q_ref.shape[1]
    nk = k_ref.shape[1]
    lane = _lane_iota(tq)
    sls = (slice(0, LANES), slice(LANES, 2 * LANES))
    s_refs = (s0_ref, s1_ref)
    ms = [_logits_pass(q_ref[0, :, sls[j]], k_ref, s_refs[j], sls[j], nk) for j in range(2)]
    outs = [_pv_pass(ms[j], v_ref, s_refs[j], sls[j], nk) for j in range(2)]
    o_ref[0] = (jnp.where(lane < HEAD_DIM, outs[0], outs[1]) * g_ref[0].astype(F32)).astype(o_ref.dtype)


def _attention_latent(qall, kall, vall, gates, n_lat):
    bsz, t, _ = qall.shape
    return pl.pallas_call(
        _attn_body,
        grid=(bsz, 4, n_lat // TQ),
        in_specs=[pl.BlockSpec((1, TQ, 2 * LANES), lambda b, p, i: (b, i, p)),
                  pl.BlockSpec((1, t, 2 * LANES), lambda b, p, i: (b, 0, p)),
                  pl.BlockSpec((1, t, 2 * LANES), lambda b, p, i: (b, 0, p)),
                  pl.BlockSpec((1, TQ, LANES), lambda b, p, i: (b, i, p))],
        out_specs=pl.BlockSpec((1, TQ, LANES), lambda b, p, i: (b, i, p)),
        out_shape=jax.ShapeDtypeStruct((bsz, n_lat, 4 * LANES), BF16),
        scratch_shapes=[pltpu.VMEM((TQ, t), F32), pltpu.VMEM((TQ, t), F32)],
        compiler_params=_cparams(("parallel", "parallel", "arbitrary")),
        name="attn_global",
    )(qall, kall, vall, gates)


def _attention_ctx(qall, kall, vall, gates, n_lat, n_ctx):
    bsz, t, _ = qall.shape
    qoff = n_lat // TQ_CTX
    koff = n_lat // n_ctx
    return pl.pallas_call(
        _attn_body,
        grid=(bsz, 4, n_ctx // TQ_CTX),
        in_specs=[pl.BlockSpec((1, TQ_CTX, 2 * LANES), lambda b, p, i: (b, qoff + i, p)),
                  pl.BlockSpec((1, n_ctx, 2 * LANES), lambda b, p, i: (b, koff, p)),
                  pl.BlockSpec((1, n_ctx, 2 * LANES), lambda b, p, i: (b, koff, p)),
                  pl.BlockSpec((1, TQ_CTX, LANES), lambda b, p, i: (b, qoff + i, p))],
        out_specs=pl.BlockSpec((1, TQ_CTX, LANES), lambda b, p, i: (b, i, p)),
        out_shape=jax.ShapeDtypeStruct((bsz, n_ctx, 4 * LANES), BF16),
        scratch_shapes=[pltpu.VMEM((TQ_CTX, n_ctx), F32), pltpu.VMEM((TQ_CTX, n_ctx), F32)],
        compiler_params=_cparams(("parallel", "parallel", "arbitrary")),
        name="attn_global_ctx",
    )(qall, kall, vall, gates)


def _swa_kernel(q_ref, k_ref, v_ref, bias_ref, sink_ref, g_ref, o_ref, *, n_lat, n_ctx):
    nb, tq = q_ref.shape[0], q_ref.shape[1]
    band = tq + 2 * WINDOW
    i = pl.program_id(1)
    lane = _lane_iota(tq)
    lo = lane < HEAD_DIM

    start = pl.multiple_of(jnp.clip(i * tq - WINDOW, 0, n_lat - band), WINDOW)
    bias = bias_ref[0]
    streams = [(bb, hh) for bb in range(nb) for hh in range(SWA_HEADS)]

    def logits_of(bb, hh):
        ps = slice((hh // 2) * LANES, (hh // 2 + 1) * LANES)
        q = q_ref[bb, :, ps]
        zero = jnp.zeros_like(q)
        qj = jnp.where(lo, q, zero) if hh % 2 == 0 else jnp.where(lo, zero, q)
        s = jnp.concatenate([_dot_nt(qj, k_ref[bb, n_lat:n_lat + n_ctx, ps]),
                             _dot_nt(qj, k_ref[bb, pl.ds(start, band), ps])], axis=1) + bias
        sink = sink_ref[hh:hh + 1, 0:1] * LOG2E
        return s, sink, jnp.maximum(jnp.max(s, axis=-1, keepdims=True), sink)

    ahead = 2
    logits = [logits_of(*streams[j]) for j in range(ahead)]
    outs = {}
    for j, (bb, hh) in enumerate(streams):
        if j + ahead < len(streams):
            logits.append(logits_of(*streams[j + ahead]))
        s, sink, m = logits[j]
        logits[j] = None
        pr = jnp.exp2(s - m).astype(BF16)
        vsl = slice(hh * LANES, (hh + 1) * LANES)
        acc = (_dot(pr[:, 0:n_ctx], v_ref[bb, n_lat:n_lat + n_ctx, vsl])
               + _dot(pr[:, n_ctx:], v_ref[bb, pl.ds(start, band), vsl]))
        outs[bb, hh] = acc / (pltpu.roll(acc, HEAD_DIM, 1) + jnp.exp2(sink - m))
    for bb in range(nb):
        for pair in range(2):
            ps = slice(pair * LANES, (pair + 1) * LANES)
            o_ref[bb, :, ps] = (jnp.where(lo, outs[bb, 2 * pair], outs[bb, 2 * pair + 1])
                                * g_ref[bb, :, ps].astype(F32)).astype(o_ref.dtype)


def _swa_bias(n_ctx):
    band = TQ_SWA + 2 * WINDOW
    r = np.arange(TQ_SWA)[:, None]
    c = np.arange(band)[None, :]
    ok = [np.abs(c - shift - r) <= WINDOW for shift in (0, WINDOW, 2 * WINDOW)] + [np.zeros((TQ_SWA, band), bool)]
    lat = np.where(np.stack(ok), 0.0, NEG_BIG).astype(np.float32)
    return jnp.asarray(np.concatenate([np.zeros((4, TQ_SWA, n_ctx), np.float32), lat], axis=-1))


def _swa(swa_in, sink_rows, gates, layer, n_lat, n_ctx):
    bsz, t, _ = swa_in.shape
    nlt = n_lat // TQ_SWA
    bias = _swa_bias(n_ctx)

    def kind(i):
        return jnp.where(i >= nlt, 3, jnp.where(i == 0, 0, jnp.where(i == nlt - 1, 2, 1)))

    return pl.pallas_call(
        functools.partial(_swa_kernel, n_lat=n_lat, n_ctx=n_ctx),
        grid=(bsz // NB_SWA, t // TQ_SWA),
        in_specs=[pl.BlockSpec((NB_SWA, TQ_SWA, 2 * LANES), lambda b, i: (b, i, 0)),
                  pl.BlockSpec((NB_SWA, t, 2 * LANES), lambda b, i: (b, 0, 1)),
                  pl.BlockSpec((NB_SWA, t, 4 * LANES), lambda b, i: (b, 0, 1)),
                  pl.BlockSpec((1,) + bias.shape[1:], lambda b, i: (kind(i), 0, 0)),
                  pl.BlockSpec((None,) + sink_rows.shape[1:], lambda b, i: (layer, 0, 0)),
                  pl.BlockSpec((NB_SWA, TQ_SWA, 2 * LANES), lambda b, i: (b, i, 2))],
        out_specs=pl.BlockSpec((NB_SWA, TQ_SWA, 2 * LANES), lambda b, i: (b, i, 0)),
        out_shape=jax.ShapeDtypeStruct((bsz, t, 2 * LANES), BF16),
        compiler_params=_cparams(("parallel", "arbitrary")),
        name="attn_window",
    )(swa_in, swa_in, swa_in, bias, sink_rows, gates)


def _conv_kernel(x_ref, prev_ref, next_ref, w_ref, b_ref, u_ref, *, lat_tiles):
    nb, tm = x_ref.shape[0], x_ref.shape[1]
    i = pl.program_id(1)
    first = (i == 0) | (i == lat_tiles)
    last = (i == lat_tiles - 1) | (i == pl.num_programs(1) - 1)
    n_ext = tm + 2 * HALO
    for bb in range(nb):
        x = x_ref[bb]
        ext = jnp.concatenate([jnp.where(first, 0.0, prev_ref[bb]), x, jnp.where(last, 0.0, next_ref[bb])], axis=0)
        acc = x * w_ref[SSD_CONV // 2:SSD_CONV // 2 + 1, :] + b_ref[...]
        for tap in range(SSD_CONV):
            if tap != SSD_CONV // 2:
                shifted = pltpu.roll(ext, (SSD_CONV // 2 - tap) % n_ext, 0)
                acc = acc + shifted[HALO:HALO + tm, :] * w_ref[tap:tap + 1, :]
        u_ref[bb] = _silu(acc).astype(u_ref.dtype)


def _conv(xbc, conv_w8, conv_b, layer, n_lat):
    bsz, t, c = xbc.shape
    per = TM // HALO
    nh = t // HALO
    return pl.pallas_call(
        functools.partial(_conv_kernel, lat_tiles=n_lat // TM),
        grid=(bsz // NB_ROWS, t // TM),
        in_specs=[pl.BlockSpec((NB_ROWS, TM, c), lambda b, i: (b, i, 0)),
                  pl.BlockSpec((NB_ROWS, HALO, c), lambda b, i: (b, jnp.maximum(i * per - 1, 0), 0)),
                  pl.BlockSpec((NB_ROWS, HALO, c), lambda b, i: (b, jnp.minimum((i + 1) * per, nh - 1), 0)),
                  pl.BlockSpec((None,) + conv_w8.shape[1:], lambda b, i: (layer, 0, 0)),
                  pl.BlockSpec((None,) + conv_b.shape[1:], lambda b, i: (layer, 0, 0))],
        out_specs=pl.BlockSpec((NB_ROWS, TM, c), lambda b, i: (b, i, 0)),
        out_shape=jax.ShapeDtypeStruct((bsz, t, c), BF16),
        compiler_params=_cparams(("parallel", "parallel")),
        name="ssd_conv",
    )(xbc, xbc, xbc, conv_w8, conv_b)


def _ssd_kernel(uf_ref, ub_ref, dtf_ref, dtb_ref, alog_ref, pv_ref, yf_ref, yb_ref, st_ref):
    nb, q = uf_ref.shape[0], uf_ref.shape[1]
    u_refs, dt_refs, y_refs = (uf_ref, ub_ref), (dtf_ref, dtb_ref), (yf_ref, yb_ref)

    @pl.when(pl.program_id(1) == 0)
    def _():
        st_ref[...] = jnp.zeros_like(st_ref)

    lane1 = lax.broadcasted_iota(jnp.int32, (1, LANES), 1)
    avec = jnp.where(lane1 < 2 * SSD_HEADS, -jnp.exp(alog_ref[...]), 0.0)
    row = lax.broadcasted_iota(jnp.int32, (q, q), 0)
    col = lax.broadcasted_iota(jnp.int32, (q, q), 1)
    masks = (col <= row, col >= row)
    lo = _lane_iota(q) < SSD_HEAD_DIM
    lo1 = lane1 < SSD_HEAD_DIM
    scans = [(bb, d) for bb in range(nb) for d in range(2)]
    streams = [(bb, d, g) for bb, d in scans for g in range(SSD_GROUPS)]

    us, dts, a_cums, a_cum_ts, a_tots = {}, {}, {}, {}, {}
    for bb, d in scans:
        us[bb, d] = u_refs[d][bb]
        dt = dt_refs[d][bb]
        dta = dt * avec
        ones = masks[d].astype(BF16)
        hi = dta.astype(BF16)
        r1 = dta - hi.astype(F32)
        mid = r1.astype(BF16)
        lo3 = (r1 - mid.astype(F32)).astype(BF16)
        a_cum = _dot(ones, hi) + _dot(ones, mid) + _dot(ones, lo3)
        dts[bb, d] = dt
        a_cums[bb, d] = a_cum
        a_cum_ts[bb, d] = a_cum.T
        a_tots[bb, d] = a_cum[q - 1:q, :] if d == 0 else a_cum[0:1, :]

    def head_col(arr, d, hh):
        c = SSD_HEADS * d + hh
        return arr[:, c:c + 1]

    bgs, cgs, gmats, xdts, acls, atls, xss = {}, {}, {}, {}, {}, {}, {}
    for bb, d, g in streams:
        u = us[bb, d]
        b0 = SSD_INNER + g * SSD_STATE
        c0 = SSD_INNER + (SSD_GROUPS + g) * SSD_STATE
        bgs[bb, d, g] = u[:, b0:b0 + SSD_STATE]
        cgs[bb, d, g] = u[:, c0:c0 + SSD_STATE]
        xss[bb, d, g] = u[:, g * LANES:(g + 1) * LANES].astype(F32)
        gmats[bb, d, g] = _dot_nt(cgs[bb, d, g], bgs[bb, d, g])
    for bb, d, g in streams:
        h0, h1 = 2 * g, 2 * g + 1
        a_cum, a_tot, dt = a_cums[bb, d], a_tots[bb, d], dts[bb, d]
        acls[bb, d, g] = jnp.where(lo, head_col(a_cum, d, h0), head_col(a_cum, d, h1))
        atls[bb, d, g] = jnp.where(lo1, head_col(a_tot, d, h0), head_col(a_tot, d, h1))
        xdts[bb, d, g] = xss[bb, d, g] * jnp.where(lo, head_col(dt, d, h0), head_col(dt, d, h1))

    intra = {}
    for bb, d, g in streams:
        xdt_b = xdts[bb, d, g].astype(BF16)
        ys = []
        for hh in (2 * g, 2 * g + 1):
            c = SSD_HEADS * d + hh
            seg = a_cums[bb, d][:, c:c + 1] - a_cum_ts[bb, d][c:c + 1, :]
            dec = jnp.exp(jnp.where(masks[d], seg, NEG_BIG))
            ys.append(_dot((gmats[bb, d, g] * dec).astype(BF16), xdt_b))
        intra[bb, d, g] = jnp.where(lo, ys[0], ys[1])

    for bb, d, g in streams:
        key = (bb, d, g)
        st = st_ref[bb, d, g]
        y = intra[key] + _dot(cgs[key], st.astype(BF16)) * jnp.exp(acls[key])
        if d == 0:
            y = y + pv_ref[0:1, g * LANES:(g + 1) * LANES] * xss[key]
        y_refs[d][bb, :, g * LANES:(g + 1) * LANES] = y.astype(y_refs[d].dtype)
        w = (xdts[key] * jnp.exp(atls[key] - acls[key])).astype(BF16)
        st_ref[bb, d, g] = st * jnp.exp(atls[key]) + _dot_tn(bgs[key], w)


def _ssd(u, dt, alog, pvec, layer, n_lat):
    bsz, t, c = u.shape
    nc = t // SSD_Q
    ncl = n_lat // SSD_Q
    ncc = nc - ncl

    def fwd(s):
        return jnp.where(s < ncc, ncl + s, s - ncc)

    def bwd(s):
        return jnp.where(s < ncc, nc - 1 - s, ncl - 1 - (s - ncc))

    layer_blk = lambda a: pl.BlockSpec((None,) + a.shape[1:], lambda b, s: (layer,) + (0,) * (a.ndim - 1))
    y_shape = jax.ShapeDtypeStruct((bsz, t, SSD_INNER), BF16)
    return pl.pallas_call(
        _ssd_kernel,
        grid=(bsz // NB_SSD, nc),
        in_specs=[pl.BlockSpec((NB_SSD, SSD_Q, c), lambda b, s: (b, fwd(s), 0)),
                  pl.BlockSpec((NB_SSD, SSD_Q, c), lambda b, s: (b, bwd(s), 0)),
                  pl.BlockSpec((NB_SSD, SSD_Q, LANES), lambda b, s: (b, fwd(s), 0)),
                  pl.BlockSpec((NB_SSD, SSD_Q, LANES), lambda b, s: (b, bwd(s), 0)),
                  layer_blk(alog), layer_blk(pvec)],
        out_specs=[pl.BlockSpec((NB_SSD, SSD_Q, SSD_INNER), lambda b, s: (b, fwd(s), 0)),
                   pl.BlockSpec((NB_SSD, SSD_Q, SSD_INNER), lambda b, s: (b, bwd(s), 0))],
        out_shape=[y_shape, y_shape],
        scratch_shapes=[pltpu.VMEM((NB_SSD, 2, SSD_GROUPS, SSD_STATE, LANES), F32)],
        compiler_params=_cparams(("parallel", "arbitrary")),
        name="ssd_scan",
    )(u, u, dt, dt, alog, pvec)


def _outproj_kernel(*refs, final, lat_tiles, with_ctx, split_x):
    refs = list(refs)
    oab_ref = refs.pop(0)
    octx_ref = refs.pop(0) if with_ctx else None
    oc_ref, yf_ref, yb_ref, g_ref, x_ref = refs[:5]
    del refs[:5]
    xctx_ref = refs.pop(0) if split_x else None
    mod_ref, pv_ref, w_ref, fw_ref, o_ref = refs
    is_ctx = pl.program_id(1) >= lat_tiles

    nb, tm = oc_ref.shape[0], oc_ref.shape[1]
    oabs, ocs, mds = [], [], []
    for bb in range(nb):
        oabs.append(jnp.where(is_ctx, octx_ref[bb], oab_ref[bb]) if with_ctx else oab_ref[bb])
        ocs.append(oc_ref[bb])
        td = (yf_ref[bb].astype(F32) + yb_ref[bb].astype(F32)) * g_ref[bb].astype(F32)
        mds.append((td * _rms(td, SSD_INNER) * pv_ref[1:2, :]).astype(BF16))
    n_ab = oab_ref.shape[2]
    n_c = n_ab + oc_ref.shape[2]
    acc = _dot(jnp.concatenate(oabs, axis=0), w_ref[0:n_ab, :])
    acc += _dot(jnp.concatenate(ocs, axis=0), w_ref[n_ab:n_c, :])
    acc += _dot(jnp.concatenate(mds, axis=0), w_ref[n_c:n_c + SSD_INNER, :])
    for bb in range(nb):
        x = jnp.where(is_ctx, xctx_ref[bb], x_ref[bb]) if split_x else x_ref[bb]
        xn = x + mod_ref[bb, 2:3, :] * acc[bb * tm:(bb + 1) * tm]
        if final:
            xn = xn * _rms(xn, xn.shape[-1]) * fw_ref[...]
        o_ref[bb] = xn


def _lat_ctx_specs(width, nlt):
    return [pl.BlockSpec((NB_ROWS, TM, width), lambda b, i: (b, jnp.minimum(i, nlt - 1), 0)),
            pl.BlockSpec((NB_ROWS, TM, width), lambda b, i: (b, jnp.maximum(i - nlt, 0), 0))]


def _outproj(oab, oab_ctx, oc, yf, yb, gates, xsrc, mods, pvec, wout, fw, layer, n_lat, final):
    bsz, t, _ = gates.shape
    d = xsrc[0].shape[-1]
    nlt = n_lat // TM
    rows = n_lat if final else t
    with_ctx = oab_ctx is not None
    split_x = len(xsrc) == 2
    tok = lambda w: pl.BlockSpec((NB_ROWS, TM, w), lambda b, i: (b, i, 0))
    layer_blk = lambda a: pl.BlockSpec((None,) + a.shape[1:], lambda b, i: (layer,) + (0,) * (a.ndim - 1))
    oab_specs = _lat_ctx_specs(4 * LANES, nlt) if with_ctx else [tok(4 * LANES)]
    x_specs = _lat_ctx_specs(d, nlt) if split_x else [tok(d)]
    return pl.pallas_call(
        functools.partial(_outproj_kernel, final=final, lat_tiles=nlt, with_ctx=with_ctx, split_x=split_x),
        grid=(bsz // NB_ROWS, rows // TM),
        in_specs=oab_specs + [tok(2 * LANES), tok(SSD_INNER), tok(SSD_INNER),
                              pl.BlockSpec((NB_ROWS, TM, SSD_INNER), lambda b, i: (b, i, 3))] + x_specs
        + [pl.BlockSpec((None, NB_ROWS, 3, d),
                        lambda b, i: (layer, jnp.where(i < nlt, b, bsz // NB_ROWS), 0, 0)),
           layer_blk(pvec), layer_blk(wout), pl.BlockSpec(fw.shape, lambda b, i: (0, 0))],
        out_specs=tok(d),
        out_shape=jax.ShapeDtypeStruct((bsz, rows, d), F32),
        compiler_params=_cparams(("parallel", "parallel")),
        name="outproj",
    )(oab, *([oab_ctx] if with_ctx else []), oc, yf, yb, gates, *xsrc, mods, pvec, wout, fw)


def _rope_tables(n_ctx, n_lat):
    rows = n_lat // GRID_W
    row = jnp.repeat(jnp.arange(rows, dtype=F32), GRID_W)
    col = jnp.tile(jnp.arange(GRID_W, dtype=F32), rows)

    def cos_sin(rot_dim):
        n_freq = rot_dim // 4
        inv_freq = ROPE_THETA ** (-jnp.arange(n_freq, dtype=F32) / n_freq)
        ang = jnp.concatenate([row[:, None] * inv_freq, col[:, None] * inv_freq], axis=-1)
        return jnp.cos(ang), jnp.sin(ang)

    def with_ctx(tab, fill):
        return jnp.concatenate([tab, jnp.full((n_ctx, LANES), fill, F32)], axis=0)

    lane = np.arange(LANES)
    cos_a, sin_a = cos_sin(MLA_ROPE)
    half = MLA_ROPE // 2
    idx = np.where((lane >= MLA_NOPE) & (lane < MLA_NOPE + MLA_ROPE), (lane - MLA_NOPE) % half, 0)
    first = (lane >= MLA_NOPE) & (lane < MLA_NOPE + half)
    second = (lane >= MLA_NOPE + half) & (lane < MLA_NOPE + MLA_ROPE)
    rot = first | second
    ca = jnp.where(rot, cos_a[:, idx], 1.0)
    sa_up = jnp.where(first, -sin_a[:, idx], 0.0)
    sa_dn = jnp.where(second, sin_a[:, idx], 0.0)
    cos_h, sin_h = cos_sin(HEAD_DIM)
    half = HEAD_DIM // 2
    idx = lane % half
    first = (lane % HEAD_DIM) < half
    cb = cos_h[:, idx]
    sb_up = jnp.where(first, -sin_h[:, idx], 0.0)
    sb_dn = jnp.where(first, 0.0, sin_h[:, idx])
    return jnp.stack([with_ctx(ca, 1.0), with_ctx(sa_up, 0.0), with_ctx(sa_dn, 0.0),
                      with_ctx(cb, 1.0), with_ctx(sb_up, 0.0), with_ctx(sb_dn, 0.0)])


def _relayout_w_in(w_in):
    depth, d, _ = w_in.shape
    kr0 = MLA_Q_RANK + MLA_KV_RANK
    dt0 = w_in.shape[-1] - 2 * SSD_HEADS
    cat = jnp.zeros((depth, d, C_END), BF16)
    for src0, src1, dst in ((0, kr0, 0), (dt0, dt0 + 2 * SSD_HEADS, C_DT), (kr0, kr0 + MLA_ROPE, C_KR + MLA_NOPE),
                            (kr0 + MLA_ROPE, dt0, C_GA)):
        cat = lax.dynamic_update_slice(cat, w_in[..., src0:src1].astype(BF16), (0, 0, dst))
    return cat


def _relayout_mla(w_q_up, w_kv_up):
    depth = w_q_up.shape[0]
    wq = w_q_up.reshape(depth, MLA_Q_RANK, MLA_HEADS, MLA_NOPE + MLA_ROPE)
    wq = jnp.pad(wq, ((0, 0), (0, 0), (0, 0), (0, LANES - MLA_NOPE - MLA_ROPE)))
    wq = wq.reshape(depth, MLA_Q_RANK, MLA_HEADS * LANES)
    wkv = w_kv_up.reshape(depth, MLA_KV_RANK, MLA_HEADS, MLA_NOPE + MLA_V)
    wk = jnp.pad(wkv[..., :MLA_NOPE], ((0, 0), (0, 0), (0, 0), (0, LANES - MLA_NOPE)))
    wk = wk.reshape(depth, MLA_KV_RANK, MLA_HEADS * LANES)
    wv = wkv[..., MLA_NOPE:].reshape(depth, MLA_KV_RANK, MLA_HEADS * MLA_V)
    return wq.astype(BF16), jnp.concatenate([wk, wv], axis=-1).astype(BF16)


def _pad_lanes(v, width):
    return jnp.pad(v, ((0, 0), (0, width - v.shape[-1])))


def kernel(x, c, ctx, c_ctx, norm_w, w_mod, b_mod, w_in, mla_q_norm, mla_kv_norm, mla_w_q_up, mla_w_kv_up,
           gqa_q_norm, gqa_k_norm, swa_sink, ssd_conv_w, ssd_conv_b, ssd_a_log, ssd_dt_bias, ssd_d,
           ssd_norm_w, w_out, final_norm_w):
    bsz, n_lat, d = x.shape
    n_ctx = ctx.shape[1]
    depth = w_in.shape[0]
    assert d == 1024 and bsz % NB_IN == 0 and bsz + NB_IN <= 8 and bsz % NB_SWA == 0 and bsz % NB_SSD == 0
    assert NB_ROWS == NB_IN
    assert n_ctx % TM == 0 and n_ctx % TQ_CTX == 0 and n_ctx % TQ_SWA == 0 and n_ctx % SSD_Q == 0
    assert n_lat % TQ == 0 and n_lat % n_ctx == 0 and n_lat % GRID_W == 0
    assert n_lat >= TQ_SWA + 2 * WINDOW

    cvec = jnp.concatenate([c, jnp.broadcast_to(c_ctx, (NB_IN, d)), jnp.zeros((8 - bsz - NB_IN, d), F32)], axis=0)
    mods = _modulation(cvec, w_mod, b_mod).reshape(depth, 8, 3, d)[:, :bsz + NB_IN]

    wcat = _relayout_w_in(w_in)
    wq, wkv = _relayout_mla(mla_w_q_up, mla_w_kv_up)
    wout = w_out.astype(BF16)
    ropes = _rope_tables(n_ctx, n_lat)
    pv_in = jnp.stack([mla_q_norm,
                       _pad_lanes(mla_kv_norm, 256),
                       jnp.tile(gqa_q_norm, (1, 4)),
                       _pad_lanes(jnp.tile(gqa_k_norm, (1, 2)), 256),
                       _pad_lanes(ssd_dt_bias.reshape(depth, -1), 256)], axis=1)
    pv_in = jnp.pad(pv_in, ((0, 0), (0, 3), (0, 0)))
    pv_out = jnp.stack([jnp.repeat(ssd_d, SSD_HEAD_DIM, axis=-1), ssd_norm_w], axis=1)
    pv_out = jnp.pad(pv_out, ((0, 0), (0, 6), (0, 0)))
    sink_rows = jnp.broadcast_to(jnp.pad(swa_sink, ((0, 0), (0, 4)))[:, :, None], (depth, 8, LANES))
    conv_w8 = jnp.pad(ssd_conv_w, ((0, 0), (0, 8 - SSD_CONV), (0, 0)))
    alog = _pad_lanes(ssd_a_log.reshape(depth, -1), LANES).reshape(depth, 1, LANES)
    conv_b = ssd_conv_b.reshape(depth, 1, -1)
    nw = norm_w.reshape(depth, 1, d)
    fw = final_norm_w.reshape(1, d)

    xsrc = (x, ctx)
    for l in range(depth):
        final = l == depth - 1
        qall, kall, vall, swa_in, gates, xbc, dt = _inproj(xsrc, mods, nw, wcat, wq, wkv, pv_in, ropes, l, n_lat)
        oab = _attention_latent(qall, kall, vall, gates, n_lat)
        oab_ctx = None if final else _attention_ctx(qall, kall, vall, gates, n_lat, n_ctx)
        oc = _swa(swa_in, sink_rows, gates, l, n_lat, n_ctx)
        u = _conv(xbc, conv_w8, conv_b, l, n_lat)
        yf, yb = _ssd(u, dt, alog, pv_out, l, n_lat)
        xsrc = (_outproj(oab, oab_ctx, oc, yf, yb, gates, xsrc, mods, pv_out, wout, fw, l, n_lat, final),)
    return xsrc[0]
```

```python
import functools

import numpy as np
import jax
import jax.numpy as jnp
from jax import lax
from jax.experimental import pallas as pl
from jax.experimental.pallas import tpu as pltpu

F32 = jnp.float32
BF16 = jnp.bfloat16

EPS = 1e-6
GRID_W = 64
ROPE_THETA = 10000.0
HEAD_DIM = 64
MLA_HEADS, MLA_NOPE, MLA_ROPE, MLA_V = 4, 64, 32, 64
MLA_Q_RANK, MLA_KV_RANK = 256, 128
GQA_HEADS, GQA_KV_HEADS = 4, 2
SWA_HEADS, SWA_KV_HEADS = 4, 2
WINDOW = 128
SSD_HEADS, SSD_HEAD_DIM, SSD_GROUPS, SSD_STATE, SSD_CONV = 4, 64, 2, 128, 5
SSD_INNER = SSD_HEADS * SSD_HEAD_DIM
SSD_CONV_DIM = SSD_INNER + 2 * SSD_GROUPS * SSD_STATE

LANES = 128
LOG2E = 1.4426950408889634
NEG_BIG = -1e30

C_CQ, C_CKV, C_KR, C_GA = 0, 256, 384, 512
C_QB, C_KB, C_VB, C_GB = 768, 1024, 1152, 1280
C_QC, C_KC, C_VC, C_GC = 1536, 1792, 1920, 2048
C_Z, C_XBC, C_END = 2304, 2560, 3328
C_DT = C_KR

TM = 256
NB_IN = 2
NB_SWA = 2
NB_SSD = 4
NB_ROWS = 4
MOD_ROWS = 8
TQ = 1024
TQ_CTX = 256
KV_CHUNK = 1024
TQ_SWA = 256
SSD_Q = 256
HALO = 8
VMEM_LIMIT = 56 * 1024 * 1024


def _cparams(sem):
    return pltpu.CompilerParams(dimension_semantics=sem, vmem_limit_bytes=VMEM_LIMIT)


def _sigmoid(x):
    return 1.0 / (1.0 + jnp.exp(-x))


def _silu(x):
    return x * _sigmoid(x)


def _rms(x, n):
    return lax.rsqrt(jnp.sum(x * x, axis=-1, keepdims=True) * (1.0 / n) + EPS)


def _dot(a, b):
    return jnp.dot(a, b, preferred_element_type=F32)


def _dot_nt(a, b):
    return lax.dot_general(a, b, (((1,), (1,)), ((), ())), preferred_element_type=F32)


def _dot_tn(a, b):
    return lax.dot_general(a, b, (((0,), (0,)), ((), ())), preferred_element_type=F32)


def _lane_iota(rows):
    return lax.broadcasted_iota(jnp.int32, (rows, LANES), 1)


def _mod_kernel(c_ref, w_ref, b_ref, o_ref):
    cs = _silu(c_ref[...])
    o_ref[0] = _dot(cs.astype(BF16), w_ref[0].astype(BF16)) + b_ref[0]


def _modulation(cvec, w_mod, b_mod):
    depth, d, d3 = w_mod.shape
    tn = 512
    return pl.pallas_call(
        _mod_kernel,
        grid=(depth, d3 // tn),
        in_specs=[pl.BlockSpec((8, d), lambda l, j: (0, 0)),
                  pl.BlockSpec((1, d, tn), lambda l, j: (l, 0, j)),
                  pl.BlockSpec((1, 1, tn), lambda l, j: (l, 0, j))],
        out_specs=pl.BlockSpec((1, 8, tn), lambda l, j: (l, 0, j)),
        out_shape=jax.ShapeDtypeStruct((depth, 8, d3), F32),
        compiler_params=_cparams(("parallel", "parallel")),
        name="modulation",
    )(cvec, w_mod, b_mod.reshape(depth, 1, d3))


def _rope(x, c, s_up, s_dn, half):
    return x * c + pltpu.roll(x, LANES - half, 1) * s_up + pltpu.roll(x, half, 1) * s_dn


def _head_rms(blk, lane):
    sq = blk * blk
    lo = lane < HEAD_DIM
    s_lo = jnp.sum(jnp.where(lo, sq, 0.0), axis=-1, keepdims=True)
    s_hi = jnp.sum(jnp.where(lo, 0.0, sq), axis=-1, keepdims=True)
    r_lo = lax.rsqrt(s_lo * (1.0 / HEAD_DIM) + EPS)
    r_hi = lax.rsqrt(s_hi * (1.0 / HEAD_DIM) + EPS)
    return jnp.where(lo, r_lo, r_hi)


def _dup_halves(blk, lane):
    sw = pltpu.roll(blk, HEAD_DIM, 1)
    lo = lane < HEAD_DIM
    return jnp.where(lo, blk, sw), jnp.where(lo, sw, blk)


def _store_v_slots(v_ref, bb, slot, pair, lo):
    v_ref[bb, :, slot * LANES:(slot + 1) * LANES] = jnp.where(lo, pair, 1.0).astype(BF16)
    v_ref[bb, :, (slot + 1) * LANES:(slot + 2) * LANES] = jnp.where(lo, 1.0, pair).astype(BF16)


def _inproj_kernel(*refs, lat_tiles, split_x):
    refs = list(refs)
    x_ref = refs.pop(0)
    xctx_ref = refs.pop(0) if split_x else None
    (mod_ref, nw_ref, wcat_ref, wq_ref, wkv_ref, pv_ref, rope_ref,
     qall_ref, kall_ref, vall_ref, swa_ref, g_ref, xbc_ref, dt_ref) = refs
    nb, tm, d = x_ref.shape
    lane = _lane_iota(tm)
    lo = lane < HEAD_DIM
    is_ctx = pl.program_id(1) >= lat_tiles

    hs = []
    for bb in range(nb):
        x = jnp.where(is_ctx, xctx_ref[bb], x_ref[bb]) if split_x else x_ref[bb]
        xn = x * _rms(x, d)
        h = (xn * nw_ref[...]) * (1.0 + mod_ref[bb, 1:2, :]) + mod_ref[bb, 0:1, :]
        hs.append(h.astype(BF16))
    hcat = jnp.concatenate(hs, axis=0)

    bounds = (C_CQ, C_QB, C_QC, C_Z, C_END)
    proj = {}

    def project(gi):
        proj[gi] = _dot(hcat, wcat_ref[:, bounds[gi]:bounds[gi + 1]])

    def cols(bb, c0, width):
        gi = max(i for i in range(4) if bounds[i] <= c0)
        off = c0 - bounds[gi]
        rows = slice(None) if bb is None else slice(bb * tm, (bb + 1) * tm)
        return proj[gi][rows, off:off + width]

    ca, sa_up, sa_dn = rope_ref[0], rope_ref[1], rope_ref[2]
    cb, sb_up, sb_dn = rope_ref[3], rope_ref[4], rope_ref[5]
    qscale_a = (MLA_NOPE + MLA_ROPE) ** -0.5 * LOG2E
    qscale_h = HEAD_DIM ** -0.5 * LOG2E
    base = MLA_HEADS * LANES

    def stage_a():
        cq = cols(None, C_CQ, MLA_Q_RANK)
        cqn = cq * _rms(cq, MLA_Q_RANK) * pv_ref[0:1, :]
        qa_all = _dot(cqn.astype(BF16), wq_ref[...])
        ckv = cols(None, C_CKV, MLA_KV_RANK)
        ckvn = ckv * _rms(ckv, MLA_KV_RANK) * pv_ref[1:2, 0:MLA_KV_RANK]
        kv_all = _dot(ckvn.astype(BF16), wkv_ref[...])
        for bb in range(nb):
            rows = slice(bb * tm, (bb + 1) * tm)
            qa, kv = qa_all[rows], kv_all[rows]
            kr = jnp.where(lane >= MLA_NOPE, _rope(cols(bb, C_KR, LANES), ca, sa_up, sa_dn, MLA_ROPE // 2), 0.0)
            for hh in range(MLA_HEADS):
                sl = slice(hh * LANES, (hh + 1) * LANES)
                qall_ref[bb, :, sl] = (_rope(qa[:, sl], ca, sa_up, sa_dn, MLA_ROPE // 2) * qscale_a).astype(BF16)
                kall_ref[bb, :, sl] = (kv[:, sl] + kr).astype(BF16)
            for pr in range(2):
                _store_v_slots(vall_ref, bb, 2 * pr, kv[:, (4 + pr) * LANES:(5 + pr) * LANES], lo)
            g_ref[bb, :, 0:256] = _silu(cols(bb, C_GA, 256)).astype(BF16)

    def stage_b(bb):
        for blk_i in range(2):
            qb = cols(bb, C_QB + blk_i * LANES, LANES)
            qb = qb * _head_rms(qb, lane) * pv_ref[2:3, 0:LANES]
            qb = _rope(qb, cb, sb_up, sb_dn, HEAD_DIM // 2) * qscale_h
            c0 = base + 2 * blk_i * LANES
            qall_ref[bb, :, c0:c0 + LANES] = jnp.where(lo, qb, 0.0).astype(BF16)
            qall_ref[bb, :, c0 + LANES:c0 + 2 * LANES] = jnp.where(lo, 0.0, qb).astype(BF16)
        kb = cols(bb, C_KB, LANES)
        kb = kb * _head_rms(kb, lane) * pv_ref[3:4, 0:LANES]
        kb = _rope(kb, cb, sb_up, sb_dn, HEAD_DIM // 2)
        k0, k1 = _dup_halves(kb, lane)
        k0 = k0.astype(BF16)
        k1 = k1.astype(BF16)
        kall_ref[bb, :, base:base + LANES] = k0
        kall_ref[bb, :, base + LANES:base + 2 * LANES] = k0
        kall_ref[bb, :, base + 2 * LANES:base + 3 * LANES] = k1
        kall_ref[bb, :, base + 3 * LANES:base + 4 * LANES] = k1
        v0, v1 = _dup_halves(cols(bb, C_VB, LANES), lane)
        _store_v_slots(vall_ref, bb, 4, v0, lo)
        _store_v_slots(vall_ref, bb, 6, v1, lo)
        g_ref[bb, :, 256:512] = _silu(cols(bb, C_GB, 256)).astype(BF16)

    def stage_c(bb):
        for blk_i in range(2):
            qc = cols(bb, C_QC + blk_i * LANES, LANES)
            swa_ref[bb, :, blk_i * LANES:(blk_i + 1) * LANES] = (
                _rope(qc, cb, sb_up, sb_dn, HEAD_DIM // 2) * qscale_h).astype(BF16)
        kc = _rope(cols(bb, C_KC, LANES), cb, sb_up, sb_dn, HEAD_DIM // 2)
        kc0, kc1 = _dup_halves(kc, lane)
        swa_ref[bb, :, 2 * LANES:3 * LANES] = kc0.astype(BF16)
        swa_ref[bb, :, 3 * LANES:4 * LANES] = kc1.astype(BF16)
        vc0, vc1 = _dup_halves(cols(bb, C_VC, LANES), lane)
        _store_v_slots(swa_ref, bb, 4, vc0, lo)
        _store_v_slots(swa_ref, bb, 6, vc1, lo)
        g_ref[bb, :, 512:768] = _silu(cols(bb, C_GC, 256)).astype(BF16)

    def stage_d(bb):
        g_ref[bb, :, 768:1024] = _silu(cols(bb, C_Z, 256)).astype(BF16)
        xbc_ref[bb] = cols(bb, C_XBC, SSD_CONV_DIM)
        dtr = cols(bb, C_DT, LANES) + pv_ref[4:5, 0:LANES]
        dt_ref[bb] = jnp.maximum(dtr, 0.0) + jnp.log(1.0 + jnp.exp(-jnp.abs(dtr)))

    project(0)
    project(1)
    stage_a()
    project(2)
    for bb in range(nb):
        stage_b(bb)
    project(3)
    for bb in range(nb):
        stage_c(bb)
    for bb in range(nb):
        stage_d(bb)


def _inproj(xsrc, mods, nw, wcat, wq, wkv, pvec, ropes, layer, n_lat):
    bsz, _, d = xsrc[0].shape
    t = ropes.shape[1]
    nlt = n_lat // TM
    split_x = len(xsrc) == 2
    grid = (bsz // NB_IN, t // TM)
    tok = lambda w: pl.BlockSpec((NB_IN, TM, w), lambda b, i: (b, i, 0))
    layer_blk = lambda a: pl.BlockSpec((None,) + a.shape[1:], lambda b, i: (layer,) + (0,) * (a.ndim - 1))
    x_specs = [tok(d)]
    if split_x:
        x_specs = [pl.BlockSpec((NB_IN, TM, d), lambda b, i: (b, jnp.minimum(i, nlt - 1), 0)),
                   pl.BlockSpec((NB_IN, TM, d), lambda b, i: (b, jnp.maximum(i - nlt, 0), 0))]
    widths = (8 * LANES, 8 * LANES, 8 * LANES, 8 * LANES, 1024, SSD_CONV_DIM, LANES)
    dtypes = (BF16, BF16, BF16, BF16, BF16, F32, F32)
    return pl.pallas_call(
        functools.partial(_inproj_kernel, lat_tiles=nlt, split_x=split_x),
        grid=grid,
        in_specs=x_specs + [
            pl.BlockSpec((None, NB_IN, 3, d), lambda b, i: (layer, jnp.where(i < nlt, b, bsz // NB_IN), 0, 0)),
            layer_blk(nw), layer_blk(wcat), layer_blk(wq), layer_blk(wkv), layer_blk(pvec),
            pl.BlockSpec((6, TM, LANES), lambda b, i: (0, i, 0))],
        out_specs=[tok(w) for w in widths],
        out_shape=[jax.ShapeDtypeStruct((bsz, t, w), dt) for w, dt in zip(widths, dtypes)],
        compiler_params=_cparams(("parallel", "parallel")),
        name="inproj",
    )(*xsrc, mods, nw, wcat, wq, wkv, pvec, ropes)


def _kv_chunks(nk):
    return [(c0, min(KV_CHUNK, nk - c0)) for c0 in range(0, nk, KV_CHUNK)]


def _logits_pass(q, k_ref, s_ref, sl, nk):
    mrun = None
    for c0, cw in _kv_chunks(nk):
        s = _dot_nt(q, k_ref[0, c0:c0 + cw, sl])
        s_ref[:, c0:c0 + cw] = s
        for t0 in range(0, cw, LANES):
            piece = s[:, t0:t0 + LANES]
            mrun = piece if mrun is None else jnp.maximum(mrun, piece)
    return jnp.max(mrun, axis=-1, keepdims=True)


def _pv_pass(m, v_ref, s_ref, sl, nk):
    acc = None
    for c0, cw in _kv_chunks(nk):
        pr = jnp.exp2(s_ref[:, c0:c0 + cw] - m).astype(BF16)
        o = _dot(pr, v_ref[0, c0:c0 + cw, sl])
        acc = o if acc is None else acc + o
    return acc / pltpu.roll(acc, HEAD_DIM, 1)


def _attn_body(q_ref, k_ref, v_ref, g_ref, o_ref, s0_ref, s1_ref):
    tq = q_ref.shape[1]
    nk = k_ref.shape[1]
    lane = _lane_iota(tq)
    sls = (slice(0, LANES), slice(LANES, 2 * LANES))
    s_refs = (s0_ref, s1_ref)
    ms = [_logits_pass(q_ref[0, :, sls[j]], k_ref, s_refs[j], sls[j], nk) for j in range(2)]
    outs = [_pv_pass(ms[j], v_ref, s_refs[j], sls[j], nk) for j in range(2)]
    o_ref[0] = (jnp.where(lane < HEAD_DIM, outs[0], outs[1]) * g_ref[0].astype(F32)).astype(o_ref.dtype)


def _attention_latent(qall, kall, vall, gates, n_lat):
    bsz, t, _ = qall.shape
    return pl.pallas_call(
        _attn_body,
        grid=(bsz, 4, n_lat // TQ),
        in_specs=[pl.BlockSpec((1, TQ, 2 * LANES), lambda b, p, i: (b, i, p)),
                  pl.BlockSpec((1, t, 2 * LANES), lambda b, p, i: (b, 0, p)),
                  pl.BlockSpec((1, t, 2 * LANES), lambda b, p, i: (b, 0, p)),
                  pl.BlockSpec((1, TQ, LANES), lambda b, p, i: (b, i, p))],
        out_specs=pl.BlockSpec((1, TQ, LANES), lambda b, p, i: (b, i, p)),
        out_shape=jax.ShapeDtypeStruct((bsz, n_lat, 4 * LANES), BF16),
        scratch_shapes=[pltpu.VMEM((TQ, t), F32), pltpu.VMEM((TQ, t), F32)],
        compiler_params=_cparams(("parallel", "parallel", "arbitrary")),
        name="attn_global",
    )(qall, kall, vall, gates)


def _attention_ctx(qall, kall, vall, gates, n_lat, n_ctx):
    bsz, t, _ = qall.shape
    qoff = n_lat // TQ_CTX
    koff = n_lat // n_ctx
    return pl.pallas_call(
        _attn_body,
        grid=(bsz, 4, n_ctx // TQ_CTX),
        in_specs=[pl.BlockSpec((1, TQ_CTX, 2 * LANES), lambda b, p, i: (b, qoff + i, p)),
                  pl.BlockSpec((1, n_ctx, 2 * LANES), lambda b, p, i: (b, koff, p)),
                  pl.BlockSpec((1, n_ctx, 2 * LANES), lambda b, p, i: (b, koff, p)),
                  pl.BlockSpec((1, TQ_CTX, LANES), lambda b, p, i: (b, qoff + i, p))],
        out_specs=pl.BlockSpec((1, TQ_CTX, LANES), lambda b, p, i: (b, i, p)),
        out_shape=jax.ShapeDtypeStruct((bsz, n_ctx, 4 * LANES), BF16),
        scratch_shapes=[pltpu.VMEM((TQ_CTX, n_ctx), F32), pltpu.VMEM((TQ_CTX, n_ctx), F32)],
        compiler_params=_cparams(("parallel", "parallel", "arbitrary")),
        name="attn_global_ctx",
    )(qall, kall, vall, gates)


def _swa_kernel(q_ref, k_ref, v_ref, bias_ref, sink_ref, g_ref, o_ref, *, n_lat, n_ctx):
    nb, tq = q_ref.shape[0], q_ref.shape[1]
    band = tq + 2 * WINDOW
    i = pl.program_id(1)
    lane = _lane_iota(tq)
    lo = lane < HEAD_DIM

    start = pl.multiple_of(jnp.clip(i * tq - WINDOW, 0, n_lat - band), WINDOW)
    bias = bias_ref[0]
    streams = [(bb, hh) for bb in range(nb) for hh in range(SWA_HEADS)]

    def logits_of(bb, hh):
        ps = slice((hh // 2) * LANES, (hh // 2 + 1) * LANES)
        q = q_ref[bb, :, ps]
        zero = jnp.zeros_like(q)
        qj = jnp.where(lo, q, zero) if hh % 2 == 0 else jnp.where(lo, zero, q)
        s = jnp.concatenate([_dot_nt(qj, k_ref[bb, n_lat:n_lat + n_ctx, ps]),
                             _dot_nt(qj, k_ref[bb, pl.ds(start, band), ps])], axis=1) + bias
        sink = sink_ref[hh:hh + 1, 0:1] * LOG2E
        return s, sink, jnp.maximum(jnp.max(s, axis=-1, keepdims=True), sink)

    ahead = 2
    logits = [logits_of(*streams[j]) for j in range(ahead)]
    outs = {}
    for j, (bb, hh) in enumerate(streams):
        if j + ahead < len(streams):
            logits.append(logits_of(*streams[j + ahead]))
        s, sink, m = logits[j]
        logits[j] = None
        pr = jnp.exp2(s - m).astype(BF16)
        vsl = slice(hh * LANES, (hh + 1) * LANES)
        acc = (_dot(pr[:, 0:n_ctx], v_ref[bb, n_lat:n_lat + n_ctx, vsl])
               + _dot(pr[:, n_ctx:], v_ref[bb, pl.ds(start, band), vsl]))
        outs[bb, hh] = acc / (pltpu.roll(acc, HEAD_DIM, 1) + jnp.exp2(sink - m))
    for bb in range(nb):
        for pair in range(2):
            ps = slice(pair * LANES, (pair + 1) * LANES)
            o_ref[bb, :, ps] = (jnp.where(lo, outs[bb, 2 * pair], outs[bb, 2 * pair + 1])
                                * g_ref[bb, :, ps].astype(F32)).astype(o_ref.dtype)


def _swa_bias(n_ctx):
    band = TQ_SWA + 2 * WINDOW
    r = np.arange(TQ_SWA)[:, None]
    c = np.arange(band)[None, :]
    ok = [np.abs(c - shift - r) <= WINDOW for shift in (0, WINDOW, 2 * WINDOW)] + [np.zeros((TQ_SWA, band), bool)]
    lat = np.where(np.stack(ok), 0.0, NEG_BIG).astype(np.float32)
    return jnp.asarray(np.concatenate([np.zeros((4, TQ_SWA, n_ctx), np.float32), lat], axis=-1))


def _swa(swa_in, sink_rows, gates, layer, n_lat, n_ctx):
    bsz, t, _ = swa_in.shape
    nlt = n_lat // TQ_SWA
    bias = _swa_bias(n_ctx)

    def kind(i):
        return jnp.where(i >= nlt, 3, jnp.where(i == 0, 0, jnp.where(i == nlt - 1, 2, 1)))

    return pl.pallas_call(
        functools.partial(_swa_kernel, n_lat=n_lat, n_ctx=n_ctx),
        grid=(bsz // NB_SWA, t // TQ_SWA),
        in_specs=[pl.BlockSpec((NB_SWA, TQ_SWA, 2 * LANES), lambda b, i: (b, i, 0)),
                  pl.BlockSpec((NB_SWA, t, 2 * LANES), lambda b, i: (b, 0, 1)),
                  pl.BlockSpec((NB_SWA, t, 4 * LANES), lambda b, i: (b, 0, 1)),
                  pl.BlockSpec((1,) + bias.shape[1:], lambda b, i: (kind(i), 0, 0)),
                  pl.BlockSpec((None,) + sink_rows.shape[1:], lambda b, i: (layer, 0, 0)),
                  pl.BlockSpec((NB_SWA, TQ_SWA, 2 * LANES), lambda b, i: (b, i, 2))],
        out_specs=pl.BlockSpec((NB_SWA, TQ_SWA, 2 * LANES), lambda b, i: (b, i, 0)),
        out_shape=jax.ShapeDtypeStruct((bsz, t, 2 * LANES), BF16),
        compiler_params=_cparams(("parallel", "arbitrary")),
        name="attn_window",
    )(swa_in, swa_in, swa_in, bias, sink_rows, gates)


def _conv_kernel(x_ref, prev_ref, next_ref, w_ref, b_ref, u_ref, *, lat_tiles):
    nb, tm = x_ref.shape[0], x_ref.shape[1]
    i = pl.program_id(1)
    first = (i == 0) | (i == lat_tiles)
    last = (i == lat_tiles - 1) | (i == pl.num_programs(1) - 1)
    n_ext = tm + 2 * HALO
    for bb in range(nb):
        x = x_ref[bb]
        ext = jnp.concatenate([jnp.where(first, 0.0, prev_ref[bb]), x, jnp.where(last, 0.0, next_ref[bb])], axis=0)
        acc = x * w_ref[SSD_CONV // 2:SSD_CONV // 2 + 1, :] + b_ref[...]
        for tap in range(SSD_CONV):
            if tap != SSD_CONV // 2:
                shifted = pltpu.roll(ext, (SSD_CONV // 2 - tap) % n_ext, 0)
                acc = acc + shifted[HALO:HALO + tm, :] * w_ref[tap:tap + 1, :]
        u_ref[bb] = _silu(acc).astype(u_ref.dtype)


def _conv(xbc, conv_w8, conv_b, layer, n_lat):
    bsz, t, c = xbc.shape
    per = TM // HALO
    nh = t // HALO
    return pl.pallas_call(
        functools.partial(_conv_kernel, lat_tiles=n_lat // TM),
        grid=(bsz // NB_ROWS, t // TM),
        in_specs=[pl.BlockSpec((NB_ROWS, TM, c), lambda b, i: (b, i, 0)),
                  pl.BlockSpec((NB_ROWS, HALO, c), lambda b, i: (b, jnp.maximum(i * per - 1, 0), 0)),
                  pl.BlockSpec((NB_ROWS, HALO, c), lambda b, i: (b, jnp.minimum((i + 1) * per, nh - 1), 0)),
                  pl.BlockSpec((None,) + conv_w8.shape[1:], lambda b, i: (layer, 0, 0)),
                  pl.BlockSpec((None,) + conv_b.shape[1:], lambda b, i: (layer, 0, 0))],
        out_specs=pl.BlockSpec((NB_ROWS, TM, c), lambda b, i: (b, i, 0)),
        out_shape=jax.ShapeDtypeStruct((bsz, t, c), BF16),
        compiler_params=_cparams(("parallel", "parallel")),
        name="ssd_conv",
    )(xbc, xbc, xbc, conv_w8, conv_b)


def _ssd_kernel(uf_ref, ub_ref, dtf_ref, dtb_ref, alog_ref, pv_ref, yf_ref, yb_ref, st_ref):
    nb, q = uf_ref.shape[0], uf_ref.shape[1]
    u_refs, dt_refs, y_refs = (uf_ref, ub_ref), (dtf_ref, dtb_ref), (yf_ref, yb_ref)

    @pl.when(pl.program_id(1) == 0)
    def _():
        st_ref[...] = jnp.zeros_like(st_ref)

    lane1 = lax.broadcasted_iota(jnp.int32, (1, LANES), 1)
    avec = jnp.where(lane1 < 2 * SSD_HEADS, -jnp.exp(alog_ref[...]), 0.0)
    row = lax.broadcasted_iota(jnp.int32, (q, q), 0)
    col = lax.broadcasted_iota(jnp.int32, (q, q), 1)
    masks = (col <= row, col >= row)
    lo = _lane_iota(q) < SSD_HEAD_DIM
    lo1 = lane1 < SSD_HEAD_DIM
    scans = [(bb, d) for bb in range(nb) for d in range(2)]
    streams = [(bb, d, g) for bb, d in scans for g in range(SSD_GROUPS)]

    us, dts, a_cums, a_cum_ts, a_tots = {}, {}, {}, {}, {}
    for bb, d in scans:
        us[bb, d] = u_refs[d][bb]
        dt = dt_refs[d][bb]
        dta = dt * avec
        ones = masks[d].astype(BF16)
        hi = dta.astype(BF16)
        r1 = dta - hi.astype(F32)
        mid = r1.astype(BF16)
        lo3 = (r1 - mid.astype(F32)).astype(BF16)
        a_cum = _dot(ones, hi) + _dot(ones, mid) + _dot(ones, lo3)
        dts[bb, d] = dt
        a_cums[bb, d] = a_cum
        a_cum_ts[bb, d] = a_cum.T
        a_tots[bb, d] = a_cum[q - 1:q, :] if d == 0 else a_cum[0:1, :]

    def head_col(arr, d, hh):
        c = SSD_HEADS * d + hh
        return arr[:, c:c + 1]

    bgs, cgs, gmats, xdts, acls, atls, xss = {}, {}, {}, {}, {}, {}, {}
    for bb, d, g in streams:
        u = us[bb, d]
        b0 = SSD_INNER + g * SSD_STATE
        c0 = SSD_INNER + (SSD_GROUPS + g) * SSD_STATE
        bgs[bb, d, g] = u[:, b0:b0 + SSD_STATE]
        cgs[bb, d, g] = u[:, c0:c0 + SSD_STATE]
        xss[bb, d, g] = u[:, g * LANES:(g + 1) * LANES].astype(F32)
        gmats[bb, d, g] = _dot_nt(cgs[bb, d, g], bgs[bb, d, g])
    for bb, d, g in streams:
        h0, h1 = 2 * g, 2 * g + 1
        a_cum, a_tot, dt = a_cums[bb, d], a_tots[bb, d], dts[bb, d]
        acls[bb, d, g] = jnp.where(lo, head_col(a_cum, d, h0), head_col(a_cum, d, h1))
        atls[bb, d, g] = jnp.where(lo1, head_col(a_tot, d, h0), head_col(a_tot, d, h1))
        xdts[bb, d, g] = xss[bb, d, g] * jnp.where(lo, head_col(dt, d, h0), head_col(dt, d, h1))

    intra = {}
    for bb, d, g in streams:
        xdt_b = xdts[bb, d, g].astype(BF16)
        ys = []
        for hh in (2 * g, 2 * g + 1):
            c = SSD_HEADS * d + hh
            seg = a_cums[bb, d][:, c:c + 1] - a_cum_ts[bb, d][c:c + 1, :]
            dec = jnp.exp(jnp.where(masks[d], seg, NEG_BIG))
            ys.append(_dot((gmats[bb, d, g] * dec).astype(BF16), xdt_b))
        intra[bb, d, g] = jnp.where(lo, ys[0], ys[1])

    for bb, d, g in streams:
        key = (bb, d, g)
        st = st_ref[bb, d, g]
        y = intra[key] + _dot(cgs[key], st.astype(BF16)) * jnp.exp(acls[key])
        if d == 0:
            y = y + pv_ref[0:1, g * LANES:(g + 1) * LANES] * xss[key]
        y_refs[d][bb, :, g * LANES:(g + 1) * LANES] = y.astype(y_refs[d].dtype)
        w = (xdts[key] * jnp.exp(atls[key] - acls[key])).astype(BF16)
        st_ref[bb, d, g] = st * jnp.exp(atls[key]) + _dot_tn(bgs[key], w)


def _ssd(u, dt, alog, pvec, layer, n_lat):
    bsz, t, c = u.shape
    nc = t // SSD_Q
    ncl = n_lat // SSD_Q
    ncc = nc - ncl

    def fwd(s):
        return jnp.where(s < ncc, ncl + s, s - ncc)

    def bwd(s):
        return jnp.where(s < ncc, nc - 1 - s, ncl - 1 - (s - ncc))

    layer_blk = lambda a: pl.BlockSpec((None,) + a.shape[1:], lambda b, s: (layer,) + (0,) * (a.ndim - 1))
    y_shape = jax.ShapeDtypeStruct((bsz, t, SSD_INNER), BF16)
    return pl.pallas_call(
        _ssd_kernel,
        grid=(bsz // NB_SSD, nc),
        in_specs=[pl.BlockSpec((NB_SSD, SSD_Q, c), lambda b, s: (b, fwd(s), 0)),
                  pl.BlockSpec((NB_SSD, SSD_Q, c), lambda b, s: (b, bwd(s), 0)),
                  pl.BlockSpec((NB_SSD, SSD_Q, LANES), lambda b, s: (b, fwd(s), 0)),
                  pl.BlockSpec((NB_SSD, SSD_Q, LANES), lambda b, s: (b, bwd(s), 0)),
                  layer_blk(alog), layer_blk(pvec)],
        out_specs=[pl.BlockSpec((NB_SSD, SSD_Q, SSD_INNER), lambda b, s: (b, fwd(s), 0)),
                   pl.BlockSpec((NB_SSD, SSD_Q, SSD_INNER), lambda b, s: (b, bwd(s), 0))],
        out_shape=[y_shape, y_shape],
        scratch_shapes=[pltpu.VMEM((NB_SSD, 2, SSD_GROUPS, SSD_STATE, LANES), F32)],
        compiler_params=_cparams(("parallel", "arbitrary")),
        name="ssd_scan",
    )(u, u, dt, dt, alog, pvec)


def _outproj_kernel(*refs, final, lat_tiles, with_ctx, split_x):
    refs = list(refs)
    oab_ref = refs.pop(0)
    octx_ref = refs.pop(0) if with_ctx else None
    oc_ref, yf_ref, yb_ref, g_ref, x_ref = refs[:5]
    del refs[:5]
    xctx_ref = refs.pop(0) if split_x else None
    mod_ref, pv_ref, w_ref, fw_ref, o_ref = refs
    is_ctx = pl.program_id(1) >= lat_tiles

    nb, tm = oc_ref.shape[0], oc_ref.shape[1]
    oabs, ocs, mds = [], [], []
    for bb in range(nb):
        oabs.append(jnp.where(is_ctx, octx_ref[bb], oab_ref[bb]) if with_ctx else oab_ref[bb])
        ocs.append(oc_ref[bb])
        td = (yf_ref[bb].astype(F32) + yb_ref[bb].astype(F32)) * g_ref[bb].astype(F32)
        mds.append((td * _rms(td, SSD_INNER) * pv_ref[1:2, :]).astype(BF16))
    n_ab = oab_ref.shape[2]
    n_c = n_ab + oc_ref.shape[2]
    acc = _dot(jnp.concatenate(oabs, axis=0), w_ref[0:n_ab, :])
    acc += _dot(jnp.concatenate(ocs, axis=0), w_ref[n_ab:n_c, :])
    acc += _dot(jnp.concatenate(mds, axis=0), w_ref[n_c:n_c + SSD_INNER, :])
    for bb in range(nb):
        x = jnp.where(is_ctx, xctx_ref[bb], x_ref[bb]) if split_x else x_ref[bb]
        xn = x + mod_ref[bb, 2:3, :] * acc[bb * tm:(bb + 1) * tm]
        if final:
            xn = xn * _rms(xn, xn.shape[-1]) * fw_ref[...]
        o_ref[bb] = xn


def _lat_ctx_specs(width, nlt):
    return [pl.BlockSpec((NB_ROWS, TM, width), lambda b, i: (b, jnp.minimum(i, nlt - 1), 0)),
            pl.BlockSpec((NB_ROWS, TM, width), lambda b, i: (b, jnp.maximum(i - nlt, 0), 0))]


def _outproj(oab, oab_ctx, oc, yf, yb, gates, xsrc, mods, pvec, wout, fw, layer, n_lat, final):
    bsz, t, _ = gates.shape
    d = xsrc[0].shape[-1]
    nlt = n_lat // TM
    rows = n_lat if final else t
    with_ctx = oab_ctx is not None
    split_x = len(xsrc) == 2
    tok = lambda w: pl.BlockSpec((NB_ROWS, TM, w), lambda b, i: (b, i, 0))
    layer_blk = lambda a: pl.BlockSpec((None,) + a.shape[1:], lambda b, i: (layer,) + (0,) * (a.ndim - 1))
    oab_specs = _lat_ctx_specs(4 * LANES, nlt) if with_ctx else [tok(4 * LANES)]
    x_specs = _lat_ctx_specs(d, nlt) if split_x else [tok(d)]
    return pl.pallas_call(
        functools.partial(_outproj_kernel, final=final, lat_tiles=nlt, with_ctx=with_ctx, split_x=split_x),
        grid=(bsz // NB_ROWS, rows // TM),
        in_specs=oab_specs + [tok(2 * LANES), tok(SSD_INNER), tok(SSD_INNER),
                              pl.BlockSpec((NB_ROWS, TM, SSD_INNER), lambda b, i: (b, i, 3))] + x_specs
        + [pl.BlockSpec((None, NB_ROWS, 3, d),
                        lambda b, i: (layer, jnp.where(i < nlt, b, bsz // NB_ROWS), 0, 0)),
           layer_blk(pvec), layer_blk(wout), pl.BlockSpec(fw.shape, lambda b, i: (0, 0))],
        out_specs=tok(d),
        out_shape=jax.ShapeDtypeStruct((bsz, rows, d), F32),
        compiler_params=_cparams(("parallel", "parallel")),
        name="outproj",
    )(oab, *([oab_ctx] if with_ctx else []), oc, yf, yb, gates, *xsrc, mods, pvec, wout, fw)


def _rope_tables(n_ctx, n_lat):
    rows = n_lat // GRID_W
    row = jnp.repeat(jnp.arange(rows, dtype=F32), GRID_W)
    col = jnp.tile(jnp.arange(GRID_W, dtype=F32), rows)

    def cos_sin(rot_dim):
        n_freq = rot_dim // 4
        inv_freq = ROPE_THETA ** (-jnp.arange(n_freq, dtype=F32) / n_freq)
        ang = jnp.concatenate([row[:, None] * inv_freq, col[:, None] * inv_freq], axis=-1)
        return jnp.cos(ang), jnp.sin(ang)

    def with_ctx(tab, fill):
        return jnp.concatenate([tab, jnp.full((n_ctx, LANES), fill, F32)], axis=0)

    lane = np.arange(LANES)
    cos_a, sin_a = cos_sin(MLA_ROPE)
    half = MLA_ROPE // 2
    idx = np.where((lane >= MLA_NOPE) & (lane < MLA_NOPE + MLA_ROPE), (lane - MLA_NOPE) % half, 0)
    first = (lane >= MLA_NOPE) & (lane < MLA_NOPE + half)
    second = (lane >= MLA_NOPE + half) & (lane < MLA_NOPE + MLA_ROPE)
    rot = first | second
    ca = jnp.where(rot, cos_a[:, idx], 1.0)
    sa_up = jnp.where(first, -sin_a[:, idx], 0.0)
    sa_dn = jnp.where(second, sin_a[:, idx], 0.0)
    cos_h, sin_h = cos_sin(HEAD_DIM)
    half = HEAD_DIM // 2
    idx = lane % half
    first = (lane % HEAD_DIM) < half
    cb = cos_h[:, idx]
    sb_up = jnp.where(first, -sin_h[:, idx], 0.0)
    sb_dn = jnp.where(first, 0.0, sin_h[:, idx])
    return jnp.stack([with_ctx(ca, 1.0), with_ctx(sa_up, 0.0), with_ctx(sa_dn, 0.0),
                      with_ctx(cb, 1.0), with_ctx(sb_up, 0.0), with_ctx(sb_dn, 0.0)])


def _relayout_w_in(w_in):
    depth, d, _ = w_in.shape
    kr0 = MLA_Q_RANK + MLA_KV_RANK
    dt0 = w_in.shape[-1] - 2 * SSD_HEADS
    cat = jnp.zeros((depth, d, C_END), BF16)
    for src0, src1, dst in ((0, kr0, 0), (dt0, dt0 + 2 * SSD_HEADS, C_DT), (kr0, kr0 + MLA_ROPE, C_KR + MLA_NOPE),
                            (kr0 + MLA_ROPE, dt0, C_GA)):
        cat = lax.dynamic_update_slice(cat, w_in[..., src0:src1].astype(BF16), (0, 0, dst))
    return cat


def _relayout_mla(w_q_up, w_kv_up):
    depth = w_q_up.shape[0]
    wq = w_q_up.reshape(depth, MLA_Q_RANK, MLA_HEADS, MLA_NOPE + MLA_ROPE)
    wq = jnp.pad(wq, ((0, 0), (0, 0), (0, 0), (0, LANES - MLA_NOPE - MLA_ROPE)))
    wq = wq.reshape(depth, MLA_Q_RANK, MLA_HEADS * LANES)
    wkv = w_kv_up.reshape(depth, MLA_KV_RANK, MLA_HEADS, MLA_NOPE + MLA_V)
    wk = jnp.pad(wkv[..., :MLA_NOPE], ((0, 0), (0, 0), (0, 0), (0, LANES - MLA_NOPE)))
    wk = wk.reshape(depth, MLA_KV_RANK, MLA_HEADS * LANES)
    wv = wkv[..., MLA_NOPE:].reshape(depth, MLA_KV_RANK, MLA_HEADS * MLA_V)
    return wq.astype(BF16), jnp.concatenate([wk, wv], axis=-1).astype(BF16)


def _pad_lanes(v, width):
    return jnp.pad(v, ((0, 0), (0, width - v.shape[-1])))


def kernel(x, c, ctx, c_ctx, norm_w, w_mod, b_mod, w_in, mla_q_norm, mla_kv_norm, mla_w_q_up, mla_w_kv_up,
           gqa_q_norm, gqa_k_norm, swa_sink, ssd_conv_w, ssd_conv_b, ssd_a_log, ssd_dt_bias, ssd_d,
           ssd_norm_w, w_out, final_norm_w):
    bsz, n_lat, d = x.shape
    n_ctx = ctx.shape[1]
    depth = w_in.shape[0]
    assert d == 1024 and all(bsz % nb == 0 for nb in (NB_IN, NB_SWA, NB_SSD, NB_ROWS))
    assert bsz + max(NB_IN, NB_ROWS) <= MOD_ROWS
    assert n_ctx % TM == 0 and n_ctx % TQ_CTX == 0 and n_ctx % TQ_SWA == 0 and n_ctx % SSD_Q == 0
    assert n_lat % TQ == 0 and n_lat % n_ctx == 0 and n_lat % GRID_W == 0
    assert n_lat >= TQ_SWA + 2 * WINDOW

    cvec = jnp.concatenate([c, jnp.broadcast_to(c_ctx, (MOD_ROWS - bsz, d))], axis=0)
    mods = _modulation(cvec, w_mod, b_mod).reshape(depth, MOD_ROWS, 3, d)

    wcat = _relayout_w_in(w_in)
    wq, wkv = _relayout_mla(mla_w_q_up, mla_w_kv_up)
    wout = w_out.astype(BF16)
    ropes = _rope_tables(n_ctx, n_lat)
    pv_in = jnp.stack([mla_q_norm,
                       _pad_lanes(mla_kv_norm, 256),
                       jnp.tile(gqa_q_norm, (1, 4)),
                       _pad_lanes(jnp.tile(gqa_k_norm, (1, 2)), 256),
                       _pad_lanes(ssd_dt_bias.reshape(depth, -1), 256)], axis=1)
    pv_in = jnp.pad(pv_in, ((0, 0), (0, 3), (0, 0)))
    pv_out = jnp.stack([jnp.repeat(ssd_d, SSD_HEAD_DIM, axis=-1), ssd_norm_w], axis=1)
    pv_out = jnp.pad(pv_out, ((0, 0), (0, 6), (0, 0)))
    sink_rows = jnp.broadcast_to(jnp.pad(swa_sink, ((0, 0), (0, 4)))[:, :, None], (depth, 8, LANES))
    conv_w8 = jnp.pad(ssd_conv_w, ((0, 0), (0, 8 - SSD_CONV), (0, 0)))
    alog = _pad_lanes(ssd_a_log.reshape(depth, -1), LANES).reshape(depth, 1, LANES)
    conv_b = ssd_conv_b.reshape(depth, 1, -1)
    nw = norm_w.reshape(depth, 1, d)
    fw = final_norm_w.reshape(1, d)

    xsrc = (x, ctx)
    for l in range(depth):
        final = l == depth - 1
        qall, kall, vall, swa_in, gates, xbc, dt = _inproj(xsrc, mods, nw, wcat, wq, wkv, pv_in, ropes, l, n_lat)
        oab = _attention_latent(qall, kall, vall, gates, n_lat)
        oab_ctx = None if final else _attention_ctx(qall, kall, vall, gates, n_lat, n_ctx)
        oc = _swa(swa_in, sink_rows, gates, l, n_lat, n_ctx)
        u = _conv(xbc, conv_w8, conv_b, l, n_lat)
        yf, yb = _ssd(u, dt, alog, pv_out, l, n_lat)
        xsrc = (_outproj(oab, oab_ctx, oc, yf, yb, gates, xsrc, mods, pv_out, wout, fw, l, n_lat, final),)
    return xsrc[0]
```

```python
import functools

import numpy as np
import jax
import jax.numpy as jnp
from jax import lax
from jax.experimental import pallas as pl
from jax.experimental.pallas import tpu as pltpu

F32 = jnp.float32
BF16 = jnp.bfloat16

EPS = 1e-6
GRID_W = 64
ROPE_THETA = 10000.0
HEAD_DIM = 64
MLA_HEADS, MLA_NOPE, MLA_ROPE, MLA_V = 4, 64, 32, 64
MLA_Q_RANK, MLA_KV_RANK = 256, 128
GQA_HEADS, GQA_KV_HEADS = 4, 2
SWA_HEADS, SWA_KV_HEADS = 4, 2
WINDOW = 128
SSD_HEADS, SSD_HEAD_DIM, SSD_GROUPS, SSD_STATE, SSD_CONV = 4, 64, 2, 128, 5
SSD_INNER = SSD_HEADS * SSD_HEAD_DIM
SSD_CONV_DIM = SSD_INNER + 2 * SSD_GROUPS * SSD_STATE

LANES = 128
LOG2E = 1.4426950408889634
NEG_BIG = -1e30

C_CQ, C_CKV, C_KR, C_GA = 0, 256, 384, 512
C_QB, C_KB, C_VB, C_GB = 768, 1024, 1152, 1280
C_QC, C_KC, C_VC, C_GC = 1536, 1792, 1920, 2048
C_Z, C_XBC, C_END = 2304, 2560, 3328
C_DT = C_KR

TM = 256
NB_IN = 2
NB_SWA = 4
NB_SSD = 4
NB_ROWS = 4
MOD_ROWS = 8
TQ = 1024
TQ_CTX = 256
KV_CHUNK = 1024
TQ_SWA = 256
SWA_BAND_BLOCKS = (TQ_SWA + 2 * WINDOW) // WINDOW
SSD_Q = 256
HALO = 8
VMEM_LIMIT = 56 * 1024 * 1024


def _cparams(sem):
    return pltpu.CompilerParams(dimension_semantics=sem, vmem_limit_bytes=VMEM_LIMIT)


def _sigmoid(x):
    return 1.0 / (1.0 + jnp.exp(-x))


def _silu(x):
    return x * _sigmoid(x)


def _rms(x, n):
    return lax.rsqrt(jnp.sum(x * x, axis=-1, keepdims=True) * (1.0 / n) + EPS)


def _dot(a, b):
    return jnp.dot(a, b, preferred_element_type=F32)


def _dot_nt(a, b):
    return lax.dot_general(a, b, (((1,), (1,)), ((), ())), preferred_element_type=F32)


def _dot_tn(a, b):
    return lax.dot_general(a, b, (((0,), (0,)), ((), ())), preferred_element_type=F32)


def _lane_iota(rows):
    return lax.broadcasted_iota(jnp.int32, (rows, LANES), 1)


def _mod_kernel(c_ref, w_ref, b_ref, o_ref):
    cs = _silu(c_ref[...])
    o_ref[0] = _dot(cs.astype(BF16), w_ref[0].astype(BF16)) + b_ref[0]


def _modulation(cvec, w_mod, b_mod):
    depth, d, d3 = w_mod.shape
    tn = 512
    return pl.pallas_call(
        _mod_kernel,
        grid=(depth, d3 // tn),
        in_specs=[pl.BlockSpec((8, d), lambda l, j: (0, 0)),
                  pl.BlockSpec((1, d, tn), lambda l, j: (l, 0, j)),
                  pl.BlockSpec((1, 1, tn), lambda l, j: (l, 0, j))],
        out_specs=pl.BlockSpec((1, 8, tn), lambda l, j: (l, 0, j)),
        out_shape=jax.ShapeDtypeStruct((depth, 8, d3), F32),
        compiler_params=_cparams(("parallel", "parallel")),
        name="modulation",
    )(cvec, w_mod, b_mod.reshape(depth, 1, d3))


def _rope(x, c, s_up, s_dn, half):
    return x * c + pltpu.roll(x, LANES - half, 1) * s_up + pltpu.roll(x, half, 1) * s_dn


def _head_rms(blk, lane):
    sq = blk * blk
    lo = lane < HEAD_DIM
    s_lo = jnp.sum(jnp.where(lo, sq, 0.0), axis=-1, keepdims=True)
    s_hi = jnp.sum(jnp.where(lo, 0.0, sq), axis=-1, keepdims=True)
    r_lo = lax.rsqrt(s_lo * (1.0 / HEAD_DIM) + EPS)
    r_hi = lax.rsqrt(s_hi * (1.0 / HEAD_DIM) + EPS)
    return jnp.where(lo, r_lo, r_hi)


def _dup_halves(blk, lane):
    sw = pltpu.roll(blk, HEAD_DIM, 1)
    lo = lane < HEAD_DIM
    return jnp.where(lo, blk, sw), jnp.where(lo, sw, blk)


def _store_v_slots(v_ref, bb, slot, pair, lo):
    v_ref[bb, :, slot * LANES:(slot + 1) * LANES] = jnp.where(lo, pair, 1.0).astype(BF16)
    v_ref[bb, :, (slot + 1) * LANES:(slot + 2) * LANES] = jnp.where(lo, 1.0, pair).astype(BF16)


def _inproj_kernel(*refs, lat_tiles, split_x):
    refs = list(refs)
    x_ref = refs.pop(0)
    xctx_ref = refs.pop(0) if split_x else None
    (mod_ref, nw_ref, wcat_ref, wq_ref, wkv_ref, pv_ref, rope_ref,
     qall_ref, kall_ref, vall_ref, swa_ref, g_ref, xbc_ref, dt_ref) = refs
    nb, tm, d = x_ref.shape
    lane = _lane_iota(tm)
    lo = lane < HEAD_DIM
    is_ctx = pl.program_id(1) >= lat_tiles

    hs = []
    for bb in range(nb):
        x = jnp.where(is_ctx, xctx_ref[bb], x_ref[bb]) if split_x else x_ref[bb]
        xn = x * _rms(x, d)
        h = (xn * nw_ref[...]) * (1.0 + mod_ref[bb, 1:2, :]) + mod_ref[bb, 0:1, :]
        hs.append(h.astype(BF16))
    hcat = jnp.concatenate(hs, axis=0)

    bounds = (C_CQ, C_QB, C_QC, C_Z, C_END)
    proj = {}

    def project(gi):
        proj[gi] = _dot(hcat, wcat_ref[:, bounds[gi]:bounds[gi + 1]])

    def cols(bb, c0, width):
        gi = max(i for i in range(4) if bounds[i] <= c0)
        off = c0 - bounds[gi]
        rows = slice(None) if bb is None else slice(bb * tm, (bb + 1) * tm)
        return proj[gi][rows, off:off + width]

    ca, sa_up, sa_dn = rope_ref[0], rope_ref[1], rope_ref[2]
    cb, sb_up, sb_dn = rope_ref[3], rope_ref[4], rope_ref[5]
    qscale_a = (MLA_NOPE + MLA_ROPE) ** -0.5 * LOG2E
    qscale_h = HEAD_DIM ** -0.5 * LOG2E
    base = MLA_HEADS * LANES

    def stage_a():
        cq = cols(None, C_CQ, MLA_Q_RANK)
        cqn = cq * _rms(cq, MLA_Q_RANK) * pv_ref[0:1, :]
        qa_all = _dot(cqn.astype(BF16), wq_ref[...])
        ckv = cols(None, C_CKV, MLA_KV_RANK)
        ckvn = ckv * _rms(ckv, MLA_KV_RANK) * pv_ref[1:2, 0:MLA_KV_RANK]
        kv_all = _dot(ckvn.astype(BF16), wkv_ref[...])
        for bb in range(nb):
            rows = slice(bb * tm, (bb + 1) * tm)
            qa, kv = qa_all[rows], kv_all[rows]
            kr = jnp.where(lane >= MLA_NOPE, _rope(cols(bb, C_KR, LANES), ca, sa_up, sa_dn, MLA_ROPE // 2), 0.0)
            for hh in range(MLA_HEADS):
                sl = slice(hh * LANES, (hh + 1) * LANES)
                qall_ref[bb, :, sl] = (_rope(qa[:, sl], ca, sa_up, sa_dn, MLA_ROPE // 2) * qscale_a).astype(BF16)
                kall_ref[bb, :, sl] = (kv[:, sl] + kr).astype(BF16)
            for pr in range(2):
                _store_v_slots(vall_ref, bb, 2 * pr, kv[:, (4 + pr) * LANES:(5 + pr) * LANES], lo)
            g_ref[bb, :, 0:256] = _silu(cols(bb, C_GA, 256)).astype(BF16)

    def stage_b(bb):
        for blk_i in range(2):
            qb = cols(bb, C_QB + blk_i * LANES, LANES)
            qb = qb * _head_rms(qb, lane) * pv_ref[2:3, 0:LANES]
            qb = _rope(qb, cb, sb_up, sb_dn, HEAD_DIM // 2) * qscale_h
            c0 = base + 2 * blk_i * LANES
            qall_ref[bb, :, c0:c0 + LANES] = jnp.where(lo, qb, 0.0).astype(BF16)
            qall_ref[bb, :, c0 + LANES:c0 + 2 * LANES] = jnp.where(lo, 0.0, qb).astype(BF16)
        kb = cols(bb, C_KB, LANES)
        kb = kb * _head_rms(kb, lane) * pv_ref[3:4, 0:LANES]
        kb = _rope(kb, cb, sb_up, sb_dn, HEAD_DIM // 2)
        k0, k1 = _dup_halves(kb, lane)
        k0 = k0.astype(BF16)
        k1 = k1.astype(BF16)
        kall_ref[bb, :, base:base + LANES] = k0
        kall_ref[bb, :, base + LANES:base + 2 * LANES] = k0
        kall_ref[bb, :, base + 2 * LANES:base + 3 * LANES] = k1
        kall_ref[bb, :, base + 3 * LANES:base + 4 * LANES] = k1
        v0, v1 = _dup_halves(cols(bb, C_VB, LANES), lane)
        _store_v_slots(vall_ref, bb, 4, v0, lo)
        _store_v_slots(vall_ref, bb, 6, v1, lo)
        g_ref[bb, :, 256:512] = _silu(cols(bb, C_GB, 256)).astype(BF16)

    def stage_c(bb):
        for blk_i in range(2):
            qc = cols(bb, C_QC + blk_i * LANES, LANES)
            swa_ref[bb, :, blk_i * LANES:(blk_i + 1) * LANES] = (
                _rope(qc, cb, sb_up, sb_dn, HEAD_DIM // 2) * qscale_h).astype(BF16)
        kc = _rope(cols(bb, C_KC, LANES), cb, sb_up, sb_dn, HEAD_DIM // 2)
        kc0, kc1 = _dup_halves(kc, lane)
        swa_ref[bb, :, 2 * LANES:3 * LANES] = kc0.astype(BF16)
        swa_ref[bb, :, 3 * LANES:4 * LANES] = kc1.astype(BF16)
        vc0, vc1 = _dup_halves(cols(bb, C_VC, LANES), lane)
        _store_v_slots(swa_ref, bb, 4, vc0, lo)
        _store_v_slots(swa_ref, bb, 6, vc1, lo)
        g_ref[bb, :, 512:768] = _silu(cols(bb, C_GC, 256)).astype(BF16)

    def stage_d(bb):
        g_ref[bb, :, 768:1024] = _silu(cols(bb, C_Z, 256)).astype(BF16)
        xbc_ref[bb] = cols(bb, C_XBC, SSD_CONV_DIM)
        dtr = cols(bb, C_DT, LANES) + pv_ref[4:5, 0:LANES]
        dt_ref[bb] = jnp.maximum(dtr, 0.0) + jnp.log(1.0 + jnp.exp(-jnp.abs(dtr)))

    project(0)
    project(1)
    stage_a()
    project(2)
    for bb in range(nb):
        stage_b(bb)
    project(3)
    for bb in range(nb):
        stage_c(bb)
    for bb in range(nb):
        stage_d(bb)


def _inproj(xsrc, mods, nw, wcat, wq, wkv, pvec, ropes, layer, n_lat):
    bsz, _, d = xsrc[0].shape
    t = ropes.shape[1]
    nlt = n_lat // TM
    split_x = len(xsrc) == 2
    grid = (bsz // NB_IN, t // TM)
    tok = lambda w: pl.BlockSpec((NB_IN, TM, w), lambda b, i: (b, i, 0))
    layer_blk = lambda a: pl.BlockSpec((None,) + a.shape[1:], lambda b, i: (layer,) + (0,) * (a.ndim - 1))
    x_specs = [tok(d)]
    if split_x:
        x_specs = [pl.BlockSpec((NB_IN, TM, d), lambda b, i: (b, jnp.minimum(i, nlt - 1), 0)),
                   pl.BlockSpec((NB_IN, TM, d), lambda b, i: (b, jnp.maximum(i - nlt, 0), 0))]
    widths = (8 * LANES, 8 * LANES, 8 * LANES, 8 * LANES, 1024, SSD_CONV_DIM, LANES)
    dtypes = (BF16, BF16, BF16, BF16, BF16, F32, F32)
    return pl.pallas_call(
        functools.partial(_inproj_kernel, lat_tiles=nlt, split_x=split_x),
        grid=grid,
        in_specs=x_specs + [
            pl.BlockSpec((None, NB_IN, 3, d), lambda b, i: (layer, jnp.where(i < nlt, b, bsz // NB_IN), 0, 0)),
            layer_blk(nw), layer_blk(wcat), layer_blk(wq), layer_blk(wkv), layer_blk(pvec),
            pl.BlockSpec((6, TM, LANES), lambda b, i: (0, i, 0))],
        out_specs=[tok(w) for w in widths],
        out_shape=[jax.ShapeDtypeStruct((bsz, t, w), dt) for w, dt in zip(widths, dtypes)],
        compiler_params=_cparams(("parallel", "parallel")),
        name="inproj",
    )(*xsrc, mods, nw, wcat, wq, wkv, pvec, ropes)


def _kv_chunks(nk):
    return [(c0, min(KV_CHUNK, nk - c0)) for c0 in range(0, nk, KV_CHUNK)]


def _logits_pass(q, k_ref, s_ref, sl, nk):
    mrun = None
    for c0, cw in _kv_chunks(nk):
        s = _dot_nt(q, k_ref[0, c0:c0 + cw, sl])
        s_ref[:, c0:c0 + cw] = s
        for t0 in range(0, cw, LANES):
            piece = s[:, t0:t0 + LANES]
            mrun = piece if mrun is None else jnp.maximum(mrun, piece)
    return jnp.max(mrun, axis=-1, keepdims=True)


def _pv_pass(m, v_ref, s_ref, sl, nk):
    acc = None
    for c0, cw in _kv_chunks(nk):
        pr = jnp.exp2(s_ref[:, c0:c0 + cw] - m).astype(BF16)
        o = _dot(pr, v_ref[0, c0:c0 + cw, sl])
        acc = o if acc is None else acc + o
    return acc / pltpu.roll(acc, HEAD_DIM, 1)


def _attn_body(q_ref, k_ref, v_ref, g_ref, o_ref, s0_ref, s1_ref):
    tq = q_ref.shape[1]
    nk = k_ref.shape[1]
    lane = _lane_iota(tq)
    sls = (slice(0, LANES), slice(LANES, 2 * LANES))
    s_refs = (s0_ref, s1_ref)
    ms = [_logits_pass(q_ref[0, :, sls[j]], k_ref, s_refs[j], sls[j], nk) for j in range(2)]
    outs = [_pv_pass(ms[j], v_ref, s_refs[j], sls[j], nk) for j in range(2)]
    o_ref[0] = (jnp.where(lane < HEAD_DIM, outs[0], outs[1]) * g_ref[0].astype(F32)).astype(o_ref.dtype)


def _attention_latent(qall, kall, vall, gates, n_lat):
    bsz, t, _ = qall.shape
    return pl.pallas_call(
        _attn_body,
        grid=(bsz, 4, n_lat // TQ),
        in_specs=[pl.BlockSpec((1, TQ, 2 * LANES), lambda b, p, i: (b, i, p)),
                  pl.BlockSpec((1, t, 2 * LANES), lambda b, p, i: (b, 0, p)),
                  pl.BlockSpec((1, t, 2 * LANES), lambda b, p, i: (b, 0, p)),
                  pl.BlockSpec((1, TQ, LANES), lambda b, p, i: (b, i, p))],
        out_specs=pl.BlockSpec((1, TQ, LANES), lambda b, p, i: (b, i, p)),
        out_shape=jax.ShapeDtypeStruct((bsz, n_lat, 4 * LANES), BF16),
        scratch_shapes=[pltpu.VMEM((TQ, t), F32), pltpu.VMEM((TQ, t), F32)],
        compiler_params=_cparams(("parallel", "parallel", "arbitrary")),
        name="attn_global",
    )(qall, kall, vall, gates)


def _attention_ctx(qall, kall, vall, gates, n_lat, n_ctx):
    bsz, t, _ = qall.shape
    qoff = n_lat // TQ_CTX
    koff = n_lat // n_ctx
    return pl.pallas_call(
        _attn_body,
        grid=(bsz, 4, n_ctx // TQ_CTX),
        in_specs=[pl.BlockSpec((1, TQ_CTX, 2 * LANES), lambda b, p, i: (b, qoff + i, p)),
                  pl.BlockSpec((1, n_ctx, 2 * LANES), lambda b, p, i: (b, koff, p)),
                  pl.BlockSpec((1, n_ctx, 2 * LANES), lambda b, p, i: (b, koff, p)),
                  pl.BlockSpec((1, TQ_CTX, LANES), lambda b, p, i: (b, qoff + i, p))],
        out_specs=pl.BlockSpec((1, TQ_CTX, LANES), lambda b, p, i: (b, i, p)),
        out_shape=jax.ShapeDtypeStruct((bsz, n_ctx, 4 * LANES), BF16),
        scratch_shapes=[pltpu.VMEM((TQ_CTX, n_ctx), F32), pltpu.VMEM((TQ_CTX, n_ctx), F32)],
        compiler_params=_cparams(("parallel", "parallel", "arbitrary")),
        name="attn_global_ctx",
    )(qall, kall, vall, gates)


def _swa_kernel(*refs, n_ctx):
    q_ref, kc_ref = refs[0], refs[1]
    kb_refs = refs[2:2 + SWA_BAND_BLOCKS]
    vc_ref = refs[2 + SWA_BAND_BLOCKS]
    vb_refs = refs[3 + SWA_BAND_BLOCKS:3 + 2 * SWA_BAND_BLOCKS]
    bias_ref, sink_ref, g_ref, o_ref = refs[3 + 2 * SWA_BAND_BLOCKS:]
    nb, tq = q_ref.shape[0], q_ref.shape[1]
    lane = _lane_iota(tq)
    lo = lane < HEAD_DIM
    bias = bias_ref[0]
    streams = [(bb, hh) for bb in range(nb) for hh in range(SWA_HEADS)]
    kbs = [jnp.concatenate([r[bb] for r in kb_refs], axis=0) for bb in range(nb)]
    vbs = [jnp.concatenate([r[bb] for r in vb_refs], axis=0) for bb in range(nb)]

    def logits_of(bb, hh):
        ps = slice((hh // 2) * LANES, (hh // 2 + 1) * LANES)
        q = q_ref[bb, :, ps]
        zero = jnp.zeros_like(q)
        qj = jnp.where(lo, q, zero) if hh % 2 == 0 else jnp.where(lo, zero, q)
        s = jnp.concatenate([_dot_nt(qj, kc_ref[bb, :, ps]), _dot_nt(qj, kbs[bb][:, ps])], axis=1) + bias
        sink = sink_ref[hh:hh + 1, 0:1] * LOG2E
        return s, sink, jnp.maximum(jnp.max(s, axis=-1, keepdims=True), sink)

    ahead = 2
    logits = [logits_of(*streams[j]) for j in range(ahead)]
    outs = {}
    for j, (bb, hh) in enumerate(streams):
        if j + ahead < len(streams):
            logits.append(logits_of(*streams[j + ahead]))
        s, sink, m = logits[j]
        logits[j] = None
        pr = jnp.exp2(s - m).astype(BF16)
        vsl = slice(hh * LANES, (hh + 1) * LANES)
        acc = _dot(pr[:, 0:n_ctx], vc_ref[bb, :, vsl]) + _dot(pr[:, n_ctx:], vbs[bb][:, vsl])
        outs[bb, hh] = acc / (pltpu.roll(acc, HEAD_DIM, 1) + jnp.exp2(sink - m))
    for bb in range(nb):
        for pair in range(2):
            ps = slice(pair * LANES, (pair + 1) * LANES)
            o_ref[bb, :, ps] = (jnp.where(lo, outs[bb, 2 * pair], outs[bb, 2 * pair + 1])
                                * g_ref[bb, :, ps].astype(F32)).astype(o_ref.dtype)


def _swa_bias(n_ctx):
    band = TQ_SWA + 2 * WINDOW
    r = np.arange(TQ_SWA)[:, None]
    c = np.arange(band)[None, :]
    ok = [np.abs(c - shift - r) <= WINDOW for shift in (0, WINDOW, 2 * WINDOW)] + [np.zeros((TQ_SWA, band), bool)]
    lat = np.where(np.stack(ok), 0.0, NEG_BIG).astype(np.float32)
    return jnp.asarray(np.concatenate([np.zeros((4, TQ_SWA, n_ctx), np.float32), lat], axis=-1))


def _swa(swa_in, sink_rows, gates, layer, n_lat, n_ctx):
    bsz, t, _ = swa_in.shape
    nlt = n_lat // TQ_SWA
    bias = _swa_bias(n_ctx)

    def kind(i):
        return jnp.where(i >= nlt, 3, jnp.where(i == 0, 0, jnp.where(i == nlt - 1, 2, 1)))

    def band_block(i):
        return jnp.clip(i * (TQ_SWA // WINDOW) - 1, 0, n_lat // WINDOW - SWA_BAND_BLOCKS)

    ctx_blk = n_lat // n_ctx

    def kv_specs(width):
        specs = [pl.BlockSpec((NB_SWA, n_ctx, width), lambda b, i: (b, ctx_blk, 1))]
        for j in range(SWA_BAND_BLOCKS):
            specs.append(pl.BlockSpec((NB_SWA, WINDOW, width), lambda b, i, j=j: (b, band_block(i) + j, 1)))
        return specs

    n_kv = 1 + SWA_BAND_BLOCKS
    return pl.pallas_call(
        functools.partial(_swa_kernel, n_ctx=n_ctx),
        grid=(bsz // NB_SWA, t // TQ_SWA),
        in_specs=[pl.BlockSpec((NB_SWA, TQ_SWA, 2 * LANES), lambda b, i: (b, i, 0))]
        + kv_specs(2 * LANES) + kv_specs(4 * LANES)
        + [pl.BlockSpec((1,) + bias.shape[1:], lambda b, i: (kind(i), 0, 0)),
           pl.BlockSpec((None,) + sink_rows.shape[1:], lambda b, i: (layer, 0, 0)),
           pl.BlockSpec((NB_SWA, TQ_SWA, 2 * LANES), lambda b, i: (b, i, 2))],
        out_specs=pl.BlockSpec((NB_SWA, TQ_SWA, 2 * LANES), lambda b, i: (b, i, 0)),
        out_shape=jax.ShapeDtypeStruct((bsz, t, 2 * LANES), BF16),
        compiler_params=_cparams(("parallel", "arbitrary")),
        name="attn_window",
    )(swa_in, *([swa_in] * (2 * n_kv)), bias, sink_rows, gates)


def _conv_kernel(x_ref, prev_ref, next_ref, w_ref, b_ref, u_ref, *, lat_tiles):
    nb, tm = x_ref.shape[0], x_ref.shape[1]
    i = pl.program_id(1)
    first = (i == 0) | (i == lat_tiles)
    last = (i == lat_tiles - 1) | (i == pl.num_programs(1) - 1)
    n_ext = tm + 2 * HALO
    for bb in range(nb):
        x = x_ref[bb]
        ext = jnp.concatenate([jnp.where(first, 0.0, prev_ref[bb]), x, jnp.where(last, 0.0, next_ref[bb])], axis=0)
        acc = x * w_ref[SSD_CONV // 2:SSD_CONV // 2 + 1, :] + b_ref[...]
        for tap in range(SSD_CONV):
            if tap != SSD_CONV // 2:
                shifted = pltpu.roll(ext, (SSD_CONV // 2 - tap) % n_ext, 0)
                acc = acc + shifted[HALO:HALO + tm, :] * w_ref[tap:tap + 1, :]
        u_ref[bb] = _silu(acc).astype(u_ref.dtype)


def _conv(xbc, conv_w8, conv_b, layer, n_lat):
    bsz, t, c = xbc.shape
    per = TM // HALO
    nh = t // HALO
    return pl.pallas_call(
        functools.partial(_conv_kernel, lat_tiles=n_lat // TM),
        grid=(bsz // NB_ROWS, t // TM),
        in_specs=[pl.BlockSpec((NB_ROWS, TM, c), lambda b, i: (b, i, 0)),
                  pl.BlockSpec((NB_ROWS, HALO, c), lambda b, i: (b, jnp.maximum(i * per - 1, 0), 0)),
                  pl.BlockSpec((NB_ROWS, HALO, c), lambda b, i: (b, jnp.minimum((i + 1) * per, nh - 1), 0)),
                  pl.BlockSpec((None,) + conv_w8.shape[1:], lambda b, i: (layer, 0, 0)),
                  pl.BlockSpec((None,) + conv_b.shape[1:], lambda b, i: (layer, 0, 0))],
        out_specs=pl.BlockSpec((NB_ROWS, TM, c), lambda b, i: (b, i, 0)),
        out_shape=jax.ShapeDtypeStruct((bsz, t, c), BF16),
        compiler_params=_cparams(("parallel", "parallel")),
        name="ssd_conv",
    )(xbc, xbc, xbc, conv_w8, conv_b)


def _ssd_kernel(uf_ref, ub_ref, dtf_ref, dtb_ref, alog_ref, pv_ref, yf_ref, yb_ref, st_ref):
    nb, q = uf_ref.shape[0], uf_ref.shape[1]
    u_refs, dt_refs, y_refs = (uf_ref, ub_ref), (dtf_ref, dtb_ref), (yf_ref, yb_ref)

    @pl.when(pl.program_id(1) == 0)
    def _():
        st_ref[...] = jnp.zeros_like(st_ref)

    lane1 = lax.broadcasted_iota(jnp.int32, (1, LANES), 1)
    avec = jnp.where(lane1 < 2 * SSD_HEADS, -jnp.exp(alog_ref[...]), 0.0)
    row = lax.broadcasted_iota(jnp.int32, (q, q), 0)
    col = lax.broadcasted_iota(jnp.int32, (q, q), 1)
    masks = (col <= row, col >= row)
    lo = _lane_iota(q) < SSD_HEAD_DIM
    lo1 = lane1 < SSD_HEAD_DIM
    scans = [(bb, d) for bb in range(nb) for d in range(2)]
    streams = [(bb, d, g) for bb, d in scans for g in range(SSD_GROUPS)]

    us, dts, a_cums, a_cum_ts, a_tots = {}, {}, {}, {}, {}
    for bb, d in scans:
        us[bb, d] = u_refs[d][bb]
        dt = dt_refs[d][bb]
        dta = dt * avec
        ones = masks[d].astype(BF16)
        hi = dta.astype(BF16)
        r1 = dta - hi.astype(F32)
        mid = r1.astype(BF16)
        lo3 = (r1 - mid.astype(F32)).astype(BF16)
        a_cum = _dot(ones, hi) + _dot(ones, mid) + _dot(ones, lo3)
        dts[bb, d] = dt
        a_cums[bb, d] = a_cum
        a_cum_ts[bb, d] = a_cum.T
        a_tots[bb, d] = a_cum[q - 1:q, :] if d == 0 else a_cum[0:1, :]

    def head_col(arr, d, hh):
        c = SSD_HEADS * d + hh
        return arr[:, c:c + 1]

    bgs, cgs, gmats, xdts, acls, atls, xss = {}, {}, {}, {}, {}, {}, {}
    for bb, d, g in streams:
        u = us[bb, d]
        b0 = SSD_INNER + g * SSD_STATE
        c0 = SSD_INNER + (SSD_GROUPS + g) * SSD_STATE
        bgs[bb, d, g] = u[:, b0:b0 + SSD_STATE]
        cgs[bb, d, g] = u[:, c0:c0 + SSD_STATE]
        xss[bb, d, g] = u[:, g * LANES:(g + 1) * LANES].astype(F32)
        gmats[bb, d, g] = _dot_nt(cgs[bb, d, g], bgs[bb, d, g])
    for bb, d, g in streams:
        h0, h1 = 2 * g, 2 * g + 1
        a_cum, a_tot, dt = a_cums[bb, d], a_tots[bb, d], dts[bb, d]
        acls[bb, d, g] = jnp.where(lo, head_col(a_cum, d, h0), head_col(a_cum, d, h1))
        atls[bb, d, g] = jnp.where(lo1, head_col(a_tot, d, h0), head_col(a_tot, d, h1))
        xdts[bb, d, g] = xss[bb, d, g] * jnp.where(lo, head_col(dt, d, h0), head_col(dt, d, h1))

    intra = {}
    for bb, d, g in streams:
        xdt_b = xdts[bb, d, g].astype(BF16)
        ys = []
        for hh in (2 * g, 2 * g + 1):
            c = SSD_HEADS * d + hh
            seg = a_cums[bb, d][:, c:c + 1] - a_cum_ts[bb, d][c:c + 1, :]
            dec = jnp.exp(jnp.where(masks[d], seg, NEG_BIG))
            ys.append(_dot((gmats[bb, d, g] * dec).astype(BF16), xdt_b))
        intra[bb, d, g] = jnp.where(lo, ys[0], ys[1])

    for bb, d, g in streams:
        key = (bb, d, g)
        st = st_ref[bb, d, g]
        y = intra[key] + _dot(cgs[key], st.astype(BF16)) * jnp.exp(acls[key])
        if d == 0:
            y = y + pv_ref[0:1, g * LANES:(g + 1) * LANES] * xss[key]
        y_refs[d][bb, :, g * LANES:(g + 1) * LANES] = y.astype(y_refs[d].dtype)
        w = (xdts[key] * jnp.exp(atls[key] - acls[key])).astype(BF16)
        st_ref[bb, d, g] = st * jnp.exp(atls[key]) + _dot_tn(bgs[key], w)


def _ssd(u, dt, alog, pvec, layer, n_lat):
    bsz, t, c = u.shape
    nc = t // SSD_Q
    ncl = n_lat // SSD_Q
    ncc = nc - ncl

    def fwd(s):
        return jnp.where(s < ncc, ncl + s, s - ncc)

    def bwd(s):
        return jnp.where(s < ncc, nc - 1 - s, ncl - 1 - (s - ncc))

    layer_blk = lambda a: pl.BlockSpec((None,) + a.shape[1:], lambda b, s: (layer,) + (0,) * (a.ndim - 1))
    y_shape = jax.ShapeDtypeStruct((bsz, t, SSD_INNER), BF16)
    return pl.pallas_call(
        _ssd_kernel,
        grid=(bsz // NB_SSD, nc),
        in_specs=[pl.BlockSpec((NB_SSD, SSD_Q, c), lambda b, s: (b, fwd(s), 0)),
                  pl.BlockSpec((NB_SSD, SSD_Q, c), lambda b, s: (b, bwd(s), 0)),
                  pl.BlockSpec((NB_SSD, SSD_Q, LANES), lambda b, s: (b, fwd(s), 0)),
                  pl.BlockSpec((NB_SSD, SSD_Q, LANES), lambda b, s: (b, bwd(s), 0)),
                  layer_blk(alog), layer_blk(pvec)],
        out_specs=[pl.BlockSpec((NB_SSD, SSD_Q, SSD_INNER), lambda b, s: (b, fwd(s), 0)),
                   pl.BlockSpec((NB_SSD, SSD_Q, SSD_INNER), lambda b, s: (b, bwd(s), 0))],
        out_shape=[y_shape, y_shape],
        scratch_shapes=[pltpu.VMEM((NB_SSD, 2, SSD_GROUPS, SSD_STATE, LANES), F32)],
        compiler_params=_cparams(("parallel", "arbitrary")),
        name="ssd_scan",
    )(u, u, dt, dt, alog, pvec)


def _outproj_kernel(*refs, final, lat_tiles, with_ctx, split_x):
    refs = list(refs)
    oab_ref = refs.pop(0)
    octx_ref = refs.pop(0) if with_ctx else None
    oc_ref, yf_ref, yb_ref, g_ref, x_ref = refs[:5]
    del refs[:5]
    xctx_ref = refs.pop(0) if split_x else None
    mod_ref, pv_ref, w_ref, fw_ref, o_ref = refs
    is_ctx = pl.program_id(1) >= lat_tiles

    nb, tm = oc_ref.shape[0], oc_ref.shape[1]
    oabs, ocs, mds = [], [], []
    for bb in range(nb):
        oabs.append(jnp.where(is_ctx, octx_ref[bb], oab_ref[bb]) if with_ctx else oab_ref[bb])
        ocs.append(oc_ref[bb])
        td = (yf_ref[bb].astype(F32) + yb_ref[bb].astype(F32)) * g_ref[bb].astype(F32)
        mds.append((td * _rms(td, SSD_INNER) * pv_ref[1:2, :]).astype(BF16))
    n_ab = oab_ref.shape[2]
    n_c = n_ab + oc_ref.shape[2]
    acc = _dot(jnp.concatenate(oabs, axis=0), w_ref[0:n_ab, :])
    acc += _dot(jnp.concatenate(ocs, axis=0), w_ref[n_ab:n_c, :])
    acc += _dot(jnp.concatenate(mds, axis=0), w_ref[n_c:n_c + SSD_INNER, :])
    for bb in range(nb):
        x = jnp.where(is_ctx, xctx_ref[bb], x_ref[bb]) if split_x else x_ref[bb]
        xn = x + mod_ref[bb, 2:3, :] * acc[bb * tm:(bb + 1) * tm]
        if final:
            xn = xn * _rms(xn, xn.shape[-1]) * fw_ref[...]
        o_ref[bb] = xn


def _lat_ctx_specs(width, nlt):
    return [pl.BlockSpec((NB_ROWS, TM, width), lambda b, i: (b, jnp.minimum(i, nlt - 1), 0)),
            pl.BlockSpec((NB_ROWS, TM, width), lambda b, i: (b, jnp.maximum(i - nlt, 0), 0))]


def _outproj(oab, oab_ctx, oc, yf, yb, gates, xsrc, mods, pvec, wout, fw, layer, n_lat, final):
    bsz, t, _ = gates.shape
    d = xsrc[0].shape[-1]
    nlt = n_lat // TM
    rows = n_lat if final else t
    with_ctx = oab_ctx is not None
    split_x = len(xsrc) == 2
    tok = lambda w: pl.BlockSpec((NB_ROWS, TM, w), lambda b, i: (b, i, 0))
    layer_blk = lambda a: pl.BlockSpec((None,) + a.shape[1:], lambda b, i: (layer,) + (0,) * (a.ndim - 1))
    oab_specs = _lat_ctx_specs(4 * LANES, nlt) if with_ctx else [tok(4 * LANES)]
    x_specs = _lat_ctx_specs(d, nlt) if split_x else [tok(d)]
    return pl.pallas_call(
        functools.partial(_outproj_kernel, final=final, lat_tiles=nlt, with_ctx=with_ctx, split_x=split_x),
        grid=(bsz // NB_ROWS, rows // TM),
        in_specs=oab_specs + [tok(2 * LANES), tok(SSD_INNER), tok(SSD_INNER),
                              pl.BlockSpec((NB_ROWS, TM, SSD_INNER), lambda b, i: (b, i, 3))] + x_specs
        + [pl.BlockSpec((None, NB_ROWS, 3, d),
                        lambda b, i: (layer, jnp.where(i < nlt, b, bsz // NB_ROWS), 0, 0)),
           layer_blk(pvec), layer_blk(wout), pl.BlockSpec(fw.shape, lambda b, i: (0, 0))],
        out_specs=tok(d),
        out_shape=jax.ShapeDtypeStruct((bsz, rows, d), F32),
        compiler_params=_cparams(("parallel", "parallel")),
        name="outproj",
    )(oab, *([oab_ctx] if with_ctx else []), oc, yf, yb, gates, *xsrc, mods, pvec, wout, fw)


def _rope_tables(n_ctx, n_lat):
    rows = n_lat // GRID_W
    row = jnp.repeat(jnp.arange(rows, dtype=F32), GRID_W)
    col = jnp.tile(jnp.arange(GRID_W, dtype=F32), rows)

    def cos_sin(rot_dim):
        n_freq = rot_dim // 4
        inv_freq = ROPE_THETA ** (-jnp.arange(n_freq, dtype=F32) / n_freq)
        ang = jnp.concatenate([row[:, None] * inv_freq, col[:, None] * inv_freq], axis=-1)
        return jnp.cos(ang), jnp.sin(ang)

    def with_ctx(tab, fill):
        return jnp.concatenate([tab, jnp.full((n_ctx, LANES), fill, F32)], axis=0)

    lane = np.arange(LANES)
    cos_a, sin_a = cos_sin(MLA_ROPE)
    half = MLA_ROPE // 2
    idx = np.where((lane >= MLA_NOPE) & (lane < MLA_NOPE + MLA_ROPE), (lane - MLA_NOPE) % half, 0)
    first = (lane >= MLA_NOPE) & (lane < MLA_NOPE + half)
    second = (lane >= MLA_NOPE + half) & (lane < MLA_NOPE + MLA_ROPE)
    rot = first | second
    ca = jnp.where(rot, cos_a[:, idx], 1.0)
    sa_up = jnp.where(first, -sin_a[:, idx], 0.0)
    sa_dn = jnp.where(second, sin_a[:, idx], 0.0)
    cos_h, sin_h = cos_sin(HEAD_DIM)
    half = HEAD_DIM // 2
    idx = lane % half
    first = (lane % HEAD_DIM) < half
    cb = cos_h[:, idx]
    sb_up = jnp.where(first, -sin_h[:, idx], 0.0)
    sb_dn = jnp.where(first, 0.0, sin_h[:, idx])
    return jnp.stack([with_ctx(ca, 1.0), with_ctx(sa_up, 0.0), with_ctx(sa_dn, 0.0),
                      with_ctx(cb, 1.0), with_ctx(sb_up, 0.0), with_ctx(sb_dn, 0.0)])


def _relayout_w_in(w_in):
    depth, d, _ = w_in.shape
    kr0 = MLA_Q_RANK + MLA_KV_RANK
    dt0 = w_in.shape[-1] - 2 * SSD_HEADS
    cat = jnp.zeros((depth, d, C_END), BF16)
    for src0, src1, dst in ((0, kr0, 0), (dt0, dt0 + 2 * SSD_HEADS, C_DT), (kr0, kr0 + MLA_ROPE, C_KR + MLA_NOPE),
                            (kr0 + MLA_ROPE, dt0, C_GA)):
        cat = lax.dynamic_update_slice(cat, w_in[..., src0:src1].astype(BF16), (0, 0, dst))
    return cat


def _relayout_mla(w_q_up, w_kv_up):
    depth = w_q_up.shape[0]
    wq = w_q_up.reshape(depth, MLA_Q_RANK, MLA_HEADS, MLA_NOPE + MLA_ROPE)
    wq = jnp.pad(wq, ((0, 0), (0, 0), (0, 0), (0, LANES - MLA_NOPE - MLA_ROPE)))
    wq = wq.reshape(depth, MLA_Q_RANK, MLA_HEADS * LANES)
    wkv = w_kv_up.reshape(depth, MLA_KV_RANK, MLA_HEADS, MLA_NOPE + MLA_V)
    wk = jnp.pad(wkv[..., :MLA_NOPE], ((0, 0), (0, 0), (0, 0), (0, LANES - MLA_NOPE)))
    wk = wk.reshape(depth, MLA_KV_RANK, MLA_HEADS * LANES)
    wv = wkv[..., MLA_NOPE:].reshape(depth, MLA_KV_RANK, MLA_HEADS * MLA_V)
    return wq.astype(BF16), jnp.concatenate([wk, wv], axis=-1).astype(BF16)


def _pad_lanes(v, width):
    return jnp.pad(v, ((0, 0), (0, width - v.shape[-1])))


def kernel(x, c, ctx, c_ctx, norm_w, w_mod, b_mod, w_in, mla_q_norm, mla_kv_norm, mla_w_q_up, mla_w_kv_up,
           gqa_q_norm, gqa_k_norm, swa_sink, ssd_conv_w, ssd_conv_b, ssd_a_log, ssd_dt_bias, ssd_d,
           ssd_norm_w, w_out, final_norm_w):
    bsz, n_lat, d = x.shape
    n_ctx = ctx.shape[1]
    depth = w_in.shape[0]
    assert d == 1024 and all(bsz % nb == 0 for nb in (NB_IN, NB_SWA, NB_SSD, NB_ROWS))
    assert bsz + max(NB_IN, NB_ROWS) <= MOD_ROWS
    assert n_ctx % TM == 0 and n_ctx % TQ_CTX == 0 and n_ctx % TQ_SWA == 0 and n_ctx % SSD_Q == 0
    assert n_lat % TQ == 0 and n_lat % n_ctx == 0 and n_lat % GRID_W == 0
    assert n_lat >= TQ_SWA + 2 * WINDOW

    cvec = jnp.concatenate([c, jnp.broadcast_to(c_ctx, (MOD_ROWS - bsz, d))], axis=0)
    mods = _modulation(cvec, w_mod, b_mod).reshape(depth, MOD_ROWS, 3, d)

    wcat = _relayout_w_in(w_in)
    wq, wkv = _relayout_mla(mla_w_q_up, mla_w_kv_up)
    wout = w_out.astype(BF16)
    ropes = _rope_tables(n_ctx, n_lat)
    pv_in = jnp.stack([mla_q_norm,
                       _pad_lanes(mla_kv_norm, 256),
                       jnp.tile(gqa_q_norm, (1, 4)),
                       _pad_lanes(jnp.tile(gqa_k_norm, (1, 2)), 256),
                       _pad_lanes(ssd_dt_bias.reshape(depth, -1), 256)], axis=1)
    pv_in = jnp.pad(pv_in, ((0, 0), (0, 3), (0, 0)))
    pv_out = jnp.stack([jnp.repeat(ssd_d, SSD_HEAD_DIM, axis=-1), ssd_norm_w], axis=1)
    pv_out = jnp.pad(pv_out, ((0, 0), (0, 6), (0, 0)))
    sink_rows = jnp.broadcast_to(jnp.pad(swa_sink, ((0, 0), (0, 4)))[:, :, None], (depth, 8, LANES))
    conv_w8 = jnp.pad(ssd_conv_w, ((0, 0), (0, 8 - SSD_CONV), (0, 0)))
    alog = _pad_lanes(ssd_a_log.reshape(depth, -1), LANES).reshape(depth, 1, LANES)
    conv_b = ssd_conv_b.reshape(depth, 1, -1)
    nw = norm_w.reshape(depth, 1, d)
    fw = final_norm_w.reshape(1, d)

    xsrc = (x, ctx)
    for l in range(depth):
        final = l == depth - 1
        qall, kall, vall, swa_in, gates, xbc, dt = _inproj(xsrc, mods, nw, wcat, wq, wkv, pv_in, ropes, l, n_lat)
        oab = _attention_latent(qall, kall, vall, gates, n_lat)
        oab_ctx = None if final else _attention_ctx(qall, kall, vall, gates, n_lat, n_ctx)
        oc = _swa(swa_in, sink_rows, gates, l, n_lat, n_ctx)
        u = _conv(xbc, conv_w8, conv_b, l, n_lat)
        yf, yb = _ssd(u, dt, alog, pv_out, l, n_lat)
        xsrc = (_outproj(oab, oab_ctx, oc, yf, yb, gates, xsrc, mods, pv_out, wout, fw, l, n_lat, final),)
    return xsrc[0]
```
